```python
import jax, jax.numpy as jnp
from jax import lax
import numpy as np

D_MODEL = 1024
BATCH = 8
SEQ = 2048
DEPTH = 4
DEC_BATCH = 128
DEC_SEQ = 4
PAST_LEN = 16384
PAGE_SIZE = 128

POOL_WINDOWS = (2, 4, 8, 16)
POOL_GROUPS = len(POOL_WINDOWS)
POOL_WIDTH = D_MODEL
POOL_GROUP_DIM = POOL_WIDTH // POOL_GROUPS
POOL_HIST = max(POOL_WINDOWS) - 1
MLSTM_WIDTH = 2 * D_MODEL
N_HEADS = 4
HEAD_DIM = MLSTM_WIDTH // N_HEADS
CONV_W = 4
PLE_DIM = 256
CHUNK = 128
EPS = 1e-6
F_BIAS_LO = 3.0
F_BIAS_HI = 6.0
SPLIT_SIZES = (POOL_WIDTH, POOL_WIDTH, 2 * MLSTM_WIDTH, MLSTM_WIDTH, MLSTM_WIDTH, MLSTM_WIDTH, 2 * N_HEADS, 2 * D_MODEL)
N_IN = sum(SPLIT_SIZES)
SPLIT_OFFSETS = tuple(int(v) for v in np.cumsum(SPLIT_SIZES)[:-1])

kernel_name = 'hybrid_pool_mlstm_decode_step'


def rmsnorm(x, w):
    xf = x.astype(jnp.float32)
    y = xf * lax.rsqrt(jnp.mean(xf * xf, axis=-1, keepdims=True) + EPS)
    return (y * w.astype(jnp.float32)).astype(x.dtype)


def head_rmsnorm(h, w):
    y = h * lax.rsqrt(jnp.mean(h * h, axis=-1, keepdims=True) + EPS)
    return y * w.astype(jnp.float32).reshape(N_HEADS, HEAD_DIM)


def causal_conv_silu(z, hist, w, b):
    S = z.shape[1]
    full = jnp.concatenate([hist.astype(z.dtype), z], axis=1)
    out = sum(full[:, j:j + S] * w[j] for j in range(CONV_W)) + b
    return jax.nn.silu(out), full[:, -(CONV_W - 1):]


def multiscale_pool(u, hist, pos0, pool_w, pool_scale):
    B, S, _ = u.shape
    full = jnp.concatenate([hist.astype(u.dtype), u], axis=1)
    cs = jnp.pad(jnp.cumsum(full.astype(jnp.float32), axis=1), ((0, 0), (1, 0), (0, 0)))
    pos = pos0 + jnp.arange(S)
    outs = []
    for g, win in enumerate(POOL_WINDOWS):
        sl = slice(g * POOL_GROUP_DIM, (g + 1) * POOL_GROUP_DIM)
        end = cs[:, POOL_HIST + 1:POOL_HIST + 1 + S, sl]
        start = cs[:, POOL_HIST + 1 - win:POOL_HIST + 1 - win + S, sl]
        cnt = jnp.minimum(pos + 1, win).astype(jnp.float32)[None, :, None]
        outs.append((end - start) / cnt)
    pooled = jnp.concatenate(outs, axis=-1) - u.astype(jnp.float32)
    mixed = jnp.einsum('bsgc,gcd->bsgd', pooled.reshape(B, S, POOL_GROUPS, POOL_GROUP_DIM),
                       pool_w.astype(jnp.float32)).reshape(B, S, POOL_WIDTH)
    return (mixed * pool_scale.astype(jnp.float32)).astype(u.dtype), full[:, -POOL_HIST:]


def mlstm_chunkwise(q, k, v, i_pre, f_pre, c0, n0, m0):
    B, H, S, DH = q.shape
    L = CHUNK if S % CHUNK == 0 else S
    nc = S // L

    def to_chunks(a):
        return jnp.moveaxis(a.reshape(a.shape[:2] + (nc, L) + a.shape[3:]), 2, 0)

    mask = jnp.tril(jnp.ones((L, L), dtype=bool))

    def step(carry, inp):
        c, n, m = carry
        qc, kc, vc, ic, fc = inp
        b = jnp.cumsum(jax.nn.log_sigmoid(fc), axis=-1)
        log_d = jnp.where(mask, b[..., :, None] - b[..., None, :] + ic[..., None, :], -jnp.inf)
        m_inter = b + m[..., None]
        m_t = jnp.maximum(m_inter, jnp.max(log_d, axis=-1))
        d = jnp.exp(log_d - m_t[..., None])
        s = jnp.einsum('bhtd,bhsd->bhts', qc, kc) * d
        g = jnp.exp(m_inter - m_t)
        num = jnp.einsum('bhts,bhse->bhte', s, vc) + g[..., None] * jnp.einsum('bhtd,bhde->bhte', qc, c)
        den = jnp.sum(s, axis=-1) + g * jnp.einsum('bhtd,bhd->bht', qc, n)
        h = num / jnp.maximum(jnp.abs(den), jnp.exp(-m_t))[..., None]
        b_last = b[..., -1]
        m_new = m_t[..., -1]
        decay = jnp.exp(b_last + m - m_new)
        w_s = jnp.exp(b_last[..., None] - b + ic - m_new[..., None])
        kw = kc * w_s[..., None]
        c_new = decay[..., None, None] * c + jnp.einsum('bhsd,bhse->bhde', kw, vc)
        n_new = decay[..., None] * n + jnp.sum(kw, axis=2)
        return (c_new, n_new, m_new), h

    (c, n, m), hs = lax.scan(step, (c0, n0, m0), tuple(to_chunks(a) for a in (q, k, v, i_pre, f_pre)))
    h = jnp.moveaxis(hs, 0, 2).reshape(B, H, S, DH)
    return h, c, n, m


def mixer_layer(x, pe, pool_hist, conv_hist, c0, n0, m0, pos0, lw):
    (norm_w, w_in, b_if, conv_w, conv_b, pool_w, pool_scale, w_pool_down, mlstm_norm_w,
     w_mlstm_down, w_out, w_ple, ple_norm_w, w_ple_gate) = lw
    B, S, _ = x.shape
    h = rmsnorm(x, norm_w)
    pool_in, pool_z, qk_pre, v, o_pre, m_z, if_pre, gate_pre = jnp.split(h @ w_in, SPLIT_OFFSETS, axis=-1)
    a, pool_hist_new = multiscale_pool(pool_in, pool_hist, pos0, pool_w, pool_scale)
    a = (a * jax.nn.silu(pool_z)) @ w_pool_down
    qk, conv_hist_new = causal_conv_silu(qk_pre, conv_hist, conv_w, conv_b)
    q, k = jnp.split(qk, 2, axis=-1)

    def heads(t):
        return t.astype(jnp.float32).reshape(B, S, N_HEADS, HEAD_DIM).transpose(0, 2, 1, 3)

    gates = (if_pre + b_if).astype(jnp.float32).transpose(0, 2, 1)
    hc, c, n, m = mlstm_chunkwise(heads(q), heads(k) * (HEAD_DIM ** -0.5), heads(v),
                                  gates[:, :N_HEADS], gates[:, N_HEADS:],
                                  c0.astype(jnp.float32), n0.astype(jnp.float32), m0.astype(jnp.float32))
    hc = hc.transpose(0, 2, 1, 3) * jax.nn.sigmoid(o_pre.astype(jnp.float32)).reshape(B, S, N_HEADS, HEAD_DIM)
    hc = head_rmsnorm(hc, mlstm_norm_w).reshape(B, S, MLSTM_WIDTH).astype(x.dtype)
    bm = (hc * jax.nn.silu(m_z)) @ w_mlstm_down
    g_a, g_b = jnp.split(jax.nn.sigmoid(gate_pre), 2, axis=-1)
    x = x + (g_a * a + g_b * bm) @ w_out
    x = x + (pe @ w_ple) * jax.nn.sigmoid(rmsnorm(x, ple_norm_w) @ w_ple_gate)
    return x, pool_hist_new, conv_hist_new, c, n, m


def setup_inputs(seed: int = 0) -> dict:
    key = jax.random.key(seed)
    ks = jax.random.split(key, 32)

    def nrm(k, shape, s):
        return jax.random.normal(k, shape, jnp.float32) * s

    f_bias = jnp.linspace(F_BIAS_LO, F_BIAS_HI, N_HEADS, dtype=jnp.float32)[None, :] + nrm(ks[21], (DEPTH, N_HEADS), 0.1)
    i_bias = nrm(ks[22], (DEPTH, N_HEADS), 0.1)
    return {
        'x_prompt': nrm(ks[0], (BATCH, SEQ, D_MODEL), 1.0),
        'x_sample': nrm(ks[1], (DEC_BATCH, DEC_SEQ, D_MODEL), 1.0),
        'state_pool': nrm(ks[2], (DEPTH, DEC_BATCH, POOL_HIST, POOL_WIDTH), 1.0),
        'state_conv': nrm(ks[3], (DEPTH, DEC_BATCH, CONV_W - 1, 2 * MLSTM_WIDTH), 1.0),
        'state_mlstm_C': nrm(ks[4], (DEPTH, DEC_BATCH, N_HEADS, HEAD_DIM, HEAD_DIM), HEAD_DIM ** -0.5),
        'state_mlstm_n': nrm(ks[5], (DEPTH, DEC_BATCH, N_HEADS, HEAD_DIM), 0.1),
        'state_mlstm_m': nrm(ks[6], (DEPTH, DEC_BATCH, N_HEADS), 1.0),
        'p_prompt': nrm(ks[7], (DEPTH, BATCH, SEQ, PLE_DIM), 1.0),
        'p_sample': nrm(ks[8], (DEPTH, DEC_BATCH, DEC_SEQ, PLE_DIM), 1.0),
        'norm_w': 1.0 + nrm(ks[9], (DEPTH, D_MODEL), 0.05),
        'w_in': nrm(ks[10], (DEPTH, D_MODEL, N_IN), D_MODEL ** -0.5),
        'b_if': jnp.concatenate([i_bias, f_bias], axis=-1),
        'conv_w': nrm(ks[11], (DEPTH, CONV_W, 2 * MLSTM_WIDTH), CONV_W ** -0.5),
        'conv_b': nrm(ks[12], (DEPTH, 2 * MLSTM_WIDTH), 0.01),
        'pool_w': nrm(ks[13], (DEPTH, POOL_GROUPS, POOL_GROUP_DIM, POOL_GROUP_DIM), POOL_GROUP_DIM ** -0.5),
        'pool_scale': 1.0 + nrm(ks[14], (DEPTH, POOL_WIDTH), 0.05),
        'w_pool_down': nrm(ks[15], (DEPTH, POOL_WIDTH, D_MODEL), POOL_WIDTH ** -0.5),
        'mlstm_norm_w': 1.0 + nrm(ks[16], (DEPTH, MLSTM_WIDTH), 0.05),
        'w_mlstm_down': nrm(ks[17], (DEPTH, MLSTM_WIDTH, D_MODEL), MLSTM_WIDTH ** -0.5),
        'w_out': nrm(ks[18], (DEPTH, D_MODEL, D_MODEL), D_MODEL ** -0.5),
        'w_ple': nrm(ks[19], (DEPTH, PLE_DIM, D_MODEL), PLE_DIM ** -0.5),
        'ple_norm_w': 1.0 + nrm(ks[20], (DEPTH, D_MODEL), 0.05),
        'w_ple_gate': nrm(ks[23], (DEPTH, D_MODEL, D_MODEL), D_MODEL ** -0.5),
        'final_norm_w': 1.0 + nrm(ks[24], (D_MODEL,), 0.05),
    }


def reference(x_prompt, x_sample, state_pool, state_conv, state_mlstm_C, state_mlstm_n, state_mlstm_m,
              p_prompt, p_sample, norm_w, w_in, b_if, conv_w, conv_b, pool_w, pool_scale, w_pool_down,
              mlstm_norm_w, w_mlstm_down, w_out, w_ple, ple_norm_w, w_ple_gate, final_norm_w):
    B = x_prompt.shape[0]
    dt = x_prompt.dtype
    pool0 = jnp.zeros((B, POOL_HIST, POOL_WIDTH), dt)
    conv0 = jnp.zeros((B, CONV_W - 1, 2 * MLSTM_WIDTH), dt)
    c0 = jnp.zeros((B, N_HEADS, HEAD_DIM, HEAD_DIM), jnp.float32)
    n0 = jnp.zeros((B, N_HEADS, HEAD_DIM), jnp.float32)
    m0 = jnp.zeros((B, N_HEADS), jnp.float32)
    xp, xs = x_prompt, x_sample
    pool_p, pool_s, conv_p, conv_s, c_p, c_s, n_p, n_s, m_p, m_s = ([] for _ in range(10))
    for l in range(DEPTH):
        lw = (norm_w[l], w_in[l], b_if[l], conv_w[l], conv_b[l], pool_w[l], pool_scale[l], w_pool_down[l],
              mlstm_norm_w[l], w_mlstm_down[l], w_out[l], w_ple[l], ple_norm_w[l], w_ple_gate[l])
        xp, ph, ch, c, n, m = mixer_layer(xp, p_prompt[l], pool0, conv0, c0, n0, m0, 0, lw)
        pool_p.append(ph); conv_p.append(ch); c_p.append(c); n_p.append(n); m_p.append(m)
        xs, ph, ch, c, n, m = mixer_layer(xs, p_sample[l], state_pool[l], state_conv[l], state_mlstm_C[l],
                                          state_mlstm_n[l], state_mlstm_m[l], PAST_LEN, lw)
        pool_s.append(ph); conv_s.append(ch); c_s.append(c); n_s.append(n); m_s.append(m)
    y_prompt = rmsnorm(xp, final_norm_w)
    y_sample = rmsnorm(xs, final_norm_w)
    return (y_prompt, y_sample, jnp.stack(pool_p), jnp.stack(pool_s), jnp.stack(conv_p), jnp.stack(conv_s),
            jnp.stack(c_p), jnp.stack(c_s), jnp.stack(n_p), jnp.stack(n_s), jnp.stack(m_p), jnp.stack(m_s))
```

```python
import functools

import jax
import jax.numpy as jnp
from jax import lax
from jax.experimental import pallas as pl
from jax.experimental.pallas import tpu as pltpu

F32 = jnp.float32
BF16 = jnp.bfloat16

D_MODEL = 1024
DEPTH = 4
POOL_WINDOWS = (2, 4, 8, 16)
POOL_WIDTH = D_MODEL
POOL_GROUP_DIM = POOL_WIDTH // len(POOL_WINDOWS)
POOL_HIST = max(POOL_WINDOWS) - 1
MLSTM_WIDTH = 2 * D_MODEL
N_HEADS = 4
HEAD_DIM = MLSTM_WIDTH // N_HEADS
CONV_W = 4
PLE_DIM = 256
CHUNK = 128
EPS = 1e-6
PAST_LEN = 16384

POOL_IN_OFF = 0
POOL_Z_OFF = POOL_WIDTH
Q_OFF = 2 * POOL_WIDTH
K_OFF = Q_OFF + MLSTM_WIDTH
V_OFF = K_OFF + MLSTM_WIDTH
O_OFF = V_OFF + MLSTM_WIDTH
MZ_OFF = O_OFF + MLSTM_WIDTH
GATE_A_OFF = MZ_OFF + MLSTM_WIDTH
GATE_B_OFF = GATE_A_OFF + D_MODEL
N_MAIN = GATE_B_OFF + D_MODEL
IF_OFF = MZ_OFF + MLSTM_WIDTH

LANES = 128
SUBLANES = 8
POOL_HALO = 16
CONV_HALO = SUBLANES
SAMPLE_ROWS = SUBLANES
VMEM_LIMIT = 56 * 1024 * 1024


def _sigmoid(x):
    return 1.0 / (1.0 + jnp.exp(-x))


def _log_sigmoid(x):
    return jnp.minimum(x, 0.0) - jnp.log1p(jnp.exp(-jnp.abs(x)))


def _rms(x, w):
    return x * lax.rsqrt(jnp.mean(x * x, axis=-1, keepdims=True) + EPS) * w


def _dot(a, b):
    return jnp.dot(a, b, preferred_element_type=F32)


def _params(*sem):
    return pltpu.CompilerParams(dimension_semantics=sem, vmem_limit_bytes=VMEM_LIMIT)


def _inproj_kernel(x_ref, nw_ref, w_ref, wih_ref, wil_ref, bif_ref, proj_ref, gates_ref, h_ref):
    @pl.when(pl.program_id(1) == 0)
    def _():
        h = _rms(x_ref[...], nw_ref[...])
        hh = h.astype(BF16)
        hl = (h - hh.astype(F32)).astype(BF16)
        h_ref[...] = hh
        gates_ref[...] = (_dot(hh, wih_ref[...]) + _dot(hl, wih_ref[...]) + _dot(hh, wil_ref[...])
                          + bif_ref[...])

    proj_ref[...] = _dot(h_ref[...], w_ref[...])


def _inproj(x, nw, w_main, wih, wil, bif, *, tm, tn):
    t = x.shape[0]
    return pl.pallas_call(
        _inproj_kernel,
        grid=(t // tm, N_MAIN // tn),
        in_specs=[
            pl.BlockSpec((tm, D_MODEL), lambda m, n: (m, 0)),
            pl.BlockSpec((1, D_MODEL), lambda m, n: (0, 0)),
            pl.BlockSpec((D_MODEL, tn), lambda m, n: (0, n)),
            pl.BlockSpec((D_MODEL, LANES), lambda m, n: (0, 0)),
            pl.BlockSpec((D_MODEL, LANES), lambda m, n: (0, 0)),
            pl.BlockSpec((1, LANES), lambda m, n: (0, 0)),
        ],
        out_specs=[
            pl.BlockSpec((tm, tn), lambda m, n: (m, n)),
            pl.BlockSpec((tm, LANES), lambda m, n: (m, 0)),
        ],
        out_shape=[
            jax.ShapeDtypeStruct((t, N_MAIN), F32),
            jax.ShapeDtypeStruct((t, LANES), F32),
        ],
        scratch_shapes=[pltpu.VMEM((tm, D_MODEL), BF16)],
        compiler_params=_params("parallel", "arbitrary"),
        name="inproj",
    )(x, nw, w_main, wih, wil, bif)


def _pool_kernel(*refs, nb, rows, seq_tiles, pos0):
    if seq_tiles > 1:
        u_ref, z_ref, halo_ref, hist_ref, pw_ref, ps_ref, wd_ref, a_ref, xs_ref = refs
        st = pl.program_id(0) % seq_tiles
        halo = jnp.where(st == 0, hist_ref[...], halo_ref[...].reshape(1, POOL_HALO, POOL_WIDTH))
        tile_pos0 = pos0 + st * rows
    else:
        u_ref, z_ref, hist_ref, pw_ref, ps_ref, wd_ref, a_ref, xs_ref = refs
        halo = hist_ref[...]
        tile_pos0 = pos0
    xs_ref[:, 0:POOL_HALO, :] = halo
    xs_ref[:, POOL_HALO:, :] = u_ref[...].reshape(nb, rows, POOL_WIDTH)
    pos = tile_pos0 + lax.broadcasted_iota(jnp.int32, (nb, rows, POOL_GROUP_DIM), 1)
    mixed = []
    for g, win in enumerate(POOL_WINDOWS):
        cols = slice(g * POOL_GROUP_DIM, (g + 1) * POOL_GROUP_DIM)
        cur = xs_ref[:, POOL_HALO:POOL_HALO + rows, cols]
        acc = cur
        for j in range(1, win):
            acc = acc + xs_ref[:, POOL_HALO - j:POOL_HALO - j + rows, cols]
        cnt = jnp.minimum(pos + 1, win).astype(F32)
        pooled = (acc / cnt - cur).reshape(nb * rows, POOL_GROUP_DIM)
        mixed.append(_dot(pooled.astype(BF16), pw_ref[g]))
    mixed = jnp.concatenate(mixed, axis=-1) * ps_ref[...]
    z = z_ref[...]
    act = mixed * (z * _sigmoid(z))
    a_ref[...] = _dot(act.astype(BF16), wd_ref[...])


def _pool(proj, hist, pw, ps, wd, *, nb, rows, seq_tiles, pos0):
    t = proj.shape[0]
    tm = nb * rows
    in_specs = [
        pl.BlockSpec((tm, POOL_WIDTH), lambda i: (i, POOL_IN_OFF // POOL_WIDTH)),
        pl.BlockSpec((tm, POOL_WIDTH), lambda i: (i, POOL_Z_OFF // POOL_WIDTH)),
    ]
    args = [proj, proj]
    if seq_tiles > 1:
        per = rows // POOL_HALO
        in_specs.append(pl.BlockSpec((POOL_HALO, POOL_WIDTH),
                                     lambda i: (jnp.maximum(i * per - 1, 0), 0)))
        args.append(proj)
        in_specs.append(pl.BlockSpec((1, POOL_HALO, POOL_WIDTH), lambda i: (i // seq_tiles, 0, 0)))
    else:
        in_specs.append(pl.BlockSpec((nb, POOL_HALO, POOL_WIDTH), lambda i: (i, 0, 0)))
    args.append(hist)
    in_specs += [
        pl.BlockSpec((len(POOL_WINDOWS), POOL_GROUP_DIM, POOL_GROUP_DIM), lambda i: (0, 0, 0)),
        pl.BlockSpec((1, POOL_WIDTH), lambda i: (0, 0)),
        pl.BlockSpec((POOL_WIDTH, D_MODEL), lambda i: (0, 0)),
    ]
    args += [pw, ps, wd]
    return pl.pallas_call(
        functools.partial(_pool_kernel, nb=nb, rows=rows, seq_tiles=seq_tiles, pos0=pos0),
        grid=(t // tm,),
        in_specs=in_specs,
        out_specs=pl.BlockSpec((tm, D_MODEL), lambda i: (i, 0)),
        out_shape=jax.ShapeDtypeStruct((t, D_MODEL), F32),
        scratch_shapes=[pltpu.VMEM((nb, POOL_HALO + rows, POOL_WIDTH), F32)],
        compiler_params=_params("parallel"),
        name="pool",
    )(*args)


def _conv_silu(xs_ref, w_ref, b_ref, rows):
    out = b_ref[...] + xs_ref[CONV_HALO:CONV_HALO + rows, :] * w_ref[CONV_W - 1:CONV_W, :]
    for j in range(CONV_W - 1):
        start = CONV_HALO - (CONV_W - 1) + j
        out = out + xs_ref[start:start + rows, :] * w_ref[j:j + 1, :]
    return out * _sigmoid(out)


def _mlstm_kernel(*refs, rows, n_valid, nc, zero_init):
    refs = list(refs)
    q_ref, k_ref, v_ref, o_ref, mz_ref = refs[:5]
    refs = refs[5:]
    if nc > 1:
        qh_ref, kh_ref = refs[:2]
        refs = refs[2:]
    hq_ref, hk_ref, gc_ref, gr_ref, cwq_ref, cwk_ref, cbq_ref, cbk_ref, nw_ref = refs[:9]
    refs = refs[9:]
    if not zero_init:
        c0_ref, n0_ref, m0_ref = refs[:3]
        refs = refs[3:]
    hcg_ref, c_ref, n_ref, m_ref, xq_ref, xk_ref = refs

    ci = pl.program_id(2)

    @pl.when(ci == 0)
    def _():
        if zero_init:
            c_ref[...] = jnp.zeros(c_ref.shape, F32)
            n_ref[...] = jnp.zeros(n_ref.shape, F32)
            m_ref[...] = jnp.zeros(m_ref.shape, F32)
        else:
            c_ref[...] = c0_ref[...]
            n_ref[...] = n0_ref[...]
            m_ref[...] = jnp.broadcast_to(m0_ref[...], m_ref.shape)

    if nc > 1:
        first = ci == 0
        xq_ref[0:CONV_HALO, :] = jnp.where(first, hq_ref[0], qh_ref[...])
        xk_ref[0:CONV_HALO, :] = jnp.where(first, hk_ref[0], kh_ref[...])
    else:
        xq_ref[0:CONV_HALO, :] = hq_ref[0]
        xk_ref[0:CONV_HALO, :] = hk_ref[0]
    xq_ref[CONV_HALO:, :] = q_ref[...]
    xk_ref[CONV_HALO:, :] = k_ref[...]
    q = _conv_silu(xq_ref, cwq_ref, cbq_ref, rows)
    k = _conv_silu(xk_ref, cwk_ref, cbk_ref, rows) * (HEAD_DIM ** -0.5)
    v = v_ref[...]

    gc = gc_ref[0]
    gr = gr_ref[0, 0]
    i_col, f_col = gc[:, 0:1], gc[:, 1:2]
    i_row, f_row = gr[0:1, :], gr[1:2, :]
    row = lax.broadcasted_iota(jnp.int32, (rows, rows), 0)
    col = lax.broadcasted_iota(jnp.int32, (rows, rows), 1)
    causal = col <= row
    b_col = jnp.sum(jnp.where(causal, _log_sigmoid(f_row), 0.0), axis=1, keepdims=True)
    b_row = jnp.sum(jnp.where(row <= col, _log_sigmoid(f_col), 0.0), axis=0, keepdims=True)

    c = c_ref[0, 0]
    n = n_ref[0, 0]
    m = m_ref[0, 0][:, 0:1]

    log_d = jnp.where(causal, b_col - b_row + i_row, -jnp.inf)
    m_inter = b_col + m
    m_t = jnp.maximum(m_inter, jnp.max(log_d, axis=-1, keepdims=True))
    d = jnp.exp(log_d - m_t)
    qb, kb, vb = q.astype(BF16), k.astype(BF16), v.astype(BF16)
    s = lax.dot_general(qb, kb, (((1,), (1,)), ((), ())), preferred_element_type=F32) * d
    g = jnp.exp(m_inter - m_t)
    num = _dot(s.astype(BF16), vb) + g * _dot(qb, c.astype(BF16))
    den = jnp.sum(s, axis=-1, keepdims=True) + g * jnp.sum(q * n, axis=-1, keepdims=True)
    h = num / jnp.maximum(jnp.abs(den), jnp.exp(-m_t))

    b_last = b_col[n_valid - 1:n_valid, :]
    m_new = m_t[n_valid - 1:n_valid, :]
    decay = jnp.exp(b_last + m - m_new)
    w_s = jnp.exp(b_last - b_col + i_col - m_new)
    kw = k * w_s
    if n_valid < rows:
        tok = lax.broadcasted_iota(jnp.int32, (rows, HEAD_DIM), 0)
        kw = jnp.where(tok < n_valid, kw, 0.0)
    c_ref[0, 0] = decay * c + lax.dot_general(kw.astype(BF16), vb, (((0,), (0,)), ((), ())),
                                              preferred_element_type=F32)
    n_ref[0, 0] = decay * n + jnp.sum(kw, axis=0, keepdims=True)
    m_ref[0, 0] = jnp.broadcast_to(m_new, (1, LANES))

    hc = h * _sigmoid(o_ref[...])
    y = _rms(hc, nw_ref[...])
    mz = mz_ref[...]
    hcg_ref[...] = y * (mz * _sigmoid(mz))


def _mlstm(proj, hist8, gc, gr, cw, cb, nw, state, *, nseq, rows, n_valid, nc):
    t = proj.shape[0]
    zero_init = state is None
    hb = HEAD_DIM

    def tok(off):
        return pl.BlockSpec((rows, hb), lambda b, h, c: (b * nc + c, off // hb + h))

    in_specs = [tok(Q_OFF), tok(K_OFF), tok(V_OFF), tok(O_OFF), tok(MZ_OFF)]
    args = [proj] * 5
    if nc > 1:
        per = rows // CONV_HALO

        def halo(off):
            return pl.BlockSpec((CONV_HALO, hb),
                                lambda b, h, c: (jnp.maximum((b * nc + c) * per - 1, 0), off // hb + h))

        in_specs += [halo(Q_OFF), halo(K_OFF)]
        args += [proj, proj]
    in_specs += [
        pl.BlockSpec((1, CONV_HALO, hb), lambda b, h, c: (b, 0, h)),
        pl.BlockSpec((1, CONV_HALO, hb), lambda b, h, c: (b, 0, N_HEADS + h)),
        pl.BlockSpec((1, rows, 2), lambda b, h, c: (h, b * nc + c, 0)),
        pl.BlockSpec((1, 1, 2, rows), lambda b, h, c: (h, b * nc + c, 0, 0)),
        pl.BlockSpec((CONV_W, hb), lambda b, h, c: (0, h)),
        pl.BlockSpec((CONV_W, hb), lambda b, h, c: (0, N_HEADS + h)),
        pl.BlockSpec((1, hb), lambda b, h, c: (0, h)),
        pl.BlockSpec((1, hb), lambda b, h, c: (0, N_HEADS + h)),
        pl.BlockSpec((1, hb), lambda b, h, c: (0, h)),
    ]
    args += [hist8, hist8, gc, gr, cw, cw, cb, cb, nw]
    c_spec = pl.BlockSpec((1, 1, hb, hb), lambda b, h, c: (b, h, 0, 0))
    n_spec = pl.BlockSpec((1, 1, 1, hb), lambda b, h, c: (b, h, 0, 0))
    if not zero_init:
        in_specs += [c_spec, n_spec, pl.BlockSpec((1, 1, 1, 1), lambda b, h, c: (b, h, 0, 0))]
        args += list(state)
    return pl.pallas_call(
        functools.partial(_mlstm_kernel, rows=rows, n_valid=n_valid, nc=nc, zero_init=zero_init),
        grid=(nseq, N_HEADS, nc),
        in_specs=in_specs,
        out_specs=[
            pl.BlockSpec((rows, hb), lambda b, h, c: (b * nc + c, h)),
            c_spec,
            n_spec,
            pl.BlockSpec((1, 1, 1, LANES), lambda b, h, c: (b, h, 0, 0)),
        ],
        out_shape=[
            jax.ShapeDtypeStruct((t, MLSTM_WIDTH), F32),
            jax.ShapeDtypeStruct((nseq, N_HEADS, hb, hb), F32),
            jax.ShapeDtypeStruct((nseq, N_HEADS, 1, hb), F32),
            jax.ShapeDtypeStruct((nseq, N_HEADS, 1, LANES), F32),
        ],
        scratch_shapes=[pltpu.VMEM((CONV_HALO + rows, hb), F32),
                        pltpu.VMEM((CONV_HALO + rows, hb), F32)],
        compiler_params=_params("parallel", "parallel", "arbitrary"),
        name="mlstm",
    )(*args)


def _merge_kernel(x_ref, a_ref, hcg_ref, ga_ref, gb_ref, pe_ref, wmd_ref, wo_ref, wple_ref, pnw_ref,
                  wpg_ref, out_ref):
    bm = _dot(hcg_ref[...].astype(BF16), wmd_ref[...])
    mix = _sigmoid(ga_ref[...]) * a_ref[...] + _sigmoid(gb_ref[...]) * bm
    x1 = x_ref[...] + _dot(mix.astype(BF16), wo_ref[...])
    r = _rms(x1, pnw_ref[...])
    gate = _sigmoid(_dot(r.astype(BF16), wpg_ref[...]))
    out_ref[...] = x1 + _dot(pe_ref[...].astype(BF16), wple_ref[...]) * gate


def _merge(x, a, hcg, proj, pe, wmd, wo, wple, pnw, wpg, *, tm):
    t = x.shape[0]

    def full(shape):
        return pl.BlockSpec(shape, lambda i: (0, 0))

    return pl.pallas_call(
        _merge_kernel,
        grid=(t // tm,),
        in_specs=[
            pl.BlockSpec((tm, D_MODEL), lambda i: (i, 0)),
            pl.BlockSpec((tm, D_MODEL), lambda i: (i, 0)),
            pl.BlockSpec((tm, MLSTM_WIDTH), lambda i: (i, 0)),
            pl.BlockSpec((tm, D_MODEL), lambda i: (i, GATE_A_OFF // D_MODEL)),
            pl.BlockSpec((tm, D_MODEL), lambda i: (i, GATE_B_OFF // D_MODEL)),
            pl.BlockSpec((tm, PLE_DIM), lambda i: (i, 0)),
            full((MLSTM_WIDTH, D_MODEL)),
            full((D_MODEL, D_MODEL)),
            full((PLE_DIM, D_MODEL)),
            full((1, D_MODEL)),
            full((D_MODEL, D_MODEL)),
        ],
        out_specs=pl.BlockSpec((tm, D_MODEL), lambda i: (i, 0)),
        out_shape=jax.ShapeDtypeStruct((t, D_MODEL), F32),
        compiler_params=_params("parallel"),
        name="merge",
    )(x, a, hcg, proj, proj, pe, wmd, wo, wple, pnw, wpg)


def _final_kernel(x_ref, w_ref, y_ref):
    y_ref[...] = _rms(x_ref[...], w_ref[...])


def _final_norm(x, w, *, tm):
    t = x.shape[0]
    return pl.pallas_call(
        _final_kernel,
        grid=(t // tm,),
        in_specs=[pl.BlockSpec((tm, D_MODEL), lambda i: (i, 0)),
                  pl.BlockSpec((1, D_MODEL), lambda i: (0, 0))],
        out_specs=pl.BlockSpec((tm, D_MODEL), lambda i: (i, 0)),
        out_shape=jax.ShapeDtypeStruct((t, D_MODEL), F32),
        compiler_params=_params("parallel"),
        name="final_norm",
    )(x, w)


def _gate_layouts(gates, rows):
    t = gates.shape[0]
    gc = gates[:, :2 * N_HEADS].reshape(t, 2, N_HEADS).transpose(2, 0, 1)
    gr = gc.reshape(N_HEADS, t // rows, rows, 2).transpose(0, 1, 3, 2)
    return gc, gr


def kernel(x_prompt, x_sample, state_pool, state_conv, state_mlstm_C, state_mlstm_n, state_mlstm_m,
           p_prompt, p_sample, norm_w, w_in, b_if, conv_w, conv_b, pool_w, pool_scale, w_pool_down,
           mlstm_norm_w, w_mlstm_down, w_out, w_ple, ple_norm_w, w_ple_gate, final_norm_w):
    batch, seq, _ = x_prompt.shape
    dec_batch, dec_seq, _ = x_sample.shape
    assert seq % CHUNK == 0 and dec_seq <= SAMPLE_ROWS and dec_seq >= CONV_W - 1
    pad_rows = SAMPLE_ROWS - dec_seq
    tp = batch * seq
    ts = dec_batch * SAMPLE_ROWS

    w_main = jnp.concatenate([w_in[:, :, :IF_OFF], w_in[:, :, IF_OFF + 2 * N_HEADS:]], axis=-1).astype(BF16)
    w_if = jnp.pad(w_in[:, :, IF_OFF:IF_OFF + 2 * N_HEADS], ((0, 0), (0, 0), (0, LANES - 2 * N_HEADS)))
    w_if_hi = w_if.astype(BF16)
    w_if_lo = (w_if - w_if_hi.astype(F32)).astype(BF16)
    b_if_p = jnp.pad(b_if, ((0, 0), (0, LANES - 2 * N_HEADS)))[:, None, :]
    pool_w_b = pool_w.astype(BF16)
    w_pool_down_b = w_pool_down.astype(BF16)
    w_mlstm_down_b = w_mlstm_down.astype(BF16)
    w_out_b = w_out.astype(BF16)
    w_ple_b = w_ple.astype(BF16)
    w_ple_gate_b = w_ple_gate.astype(BF16)

    xp = x_prompt.reshape(tp, D_MODEL)
    xs = jnp.pad(x_sample, ((0, 0), (0, pad_rows), (0, 0))).reshape(ts, D_MODEL)
    pe_s = jnp.pad(p_sample, ((0, 0), (0, 0), (0, pad_rows), (0, 0))).reshape(DEPTH, ts, PLE_DIM)
    pe_p = p_prompt.reshape(DEPTH, tp, PLE_DIM)
    hist16_s = jnp.pad(state_pool, ((0, 0), (0, 0), (POOL_HALO - POOL_HIST, 0), (0, 0)))
    hist8_s = jnp.pad(state_conv, ((0, 0), (0, 0), (CONV_HALO - (CONV_W - 1), 0), (0, 0)))
    hist16_p = jnp.zeros((batch, POOL_HALO, POOL_WIDTH), F32)
    hist8_p = jnp.zeros((batch, CONV_HALO, 2 * MLSTM_WIDTH), F32)
    n_s = state_mlstm_n[:, :, :, None, :]
    m_s = state_mlstm_m[:, :, :, None, None]

    pool_tile = 512
    outs = [[] for _ in range(10)]
    for l in range(DEPTH):
        nw = norm_w[l][None, :]
        cb = conv_b[l][None, :]
        ps = pool_scale[l][None, :]
        mnw = mlstm_norm_w[l][None, :]
        pnw = ple_norm_w[l][None, :]

        def layer(x, pe, hist16, hist8, state, *, nseq, rows, n_valid, nc, tm_in, nb, prow, seq_tiles,
                  pos0, tm_merge):
            proj, gates = _inproj(x, nw, w_main[l], w_if_hi[l], w_if_lo[l], b_if_p[l], tm=tm_in, tn=1024)
            a = _pool(proj, hist16, pool_w_b[l], ps, w_pool_down_b[l], nb=nb, rows=prow,
                      seq_tiles=seq_tiles, pos0=pos0)
            gc, gr = _gate_layouts(gates, rows)
            hcg, c, n, m = _mlstm(proj, hist8, gc, gr, conv_w[l], cb, mnw, state,
                                  nseq=nseq, rows=rows, n_valid=n_valid, nc=nc)
            x = _merge(x, a, hcg, proj, pe, w_mlstm_down_b[l], w_out_b[l], w_ple_b[l], pnw,
                       w_ple_gate_b[l], tm=tm_merge)
            return x, proj, c, n[:, :, 0, :], m[:, :, 0, 0]

        xp, proj_p, c, n, m = layer(xp, pe_p[l], hist16_p, hist8_p, None, nseq=batch, rows=CHUNK,
                                    n_valid=CHUNK, nc=seq // CHUNK, tm_in=1024, nb=1, prow=pool_tile,
                                    seq_tiles=seq // pool_tile, pos0=0, tm_merge=256)
        pp = proj_p.reshape(batch, seq, N_MAIN)
        outs[0].append(pp[:, seq - POOL_HIST:, POOL_IN_OFF:POOL_IN_OFF + POOL_WIDTH])
        outs[2].append(pp[:, seq - (CONV_W - 1):, Q_OFF:Q_OFF + 2 * MLSTM_WIDTH])
        outs[4].append(c); outs[6].append(n); outs[8].append(m)

        xs, proj_s, c, n, m = layer(xs, pe_s[l], hist16_s[l], hist8_s[l],
                                    (state_mlstm_C[l], n_s[l], m_s[l]), nseq=dec_batch,
                                    rows=SAMPLE_ROWS, n_valid=dec_seq, nc=1, tm_in=ts, nb=16,
                                    prow=SAMPLE_ROWS, seq_tiles=1, pos0=PAST_LEN, tm_merge=256)
        sp = proj_s.reshape(dec_batch, SAMPLE_ROWS, N_MAIN)
        outs[1].append(jnp.concatenate(
            [state_pool[l][:, dec_seq:], sp[:, :dec_seq, POOL_IN_OFF:POOL_IN_OFF + POOL_WIDTH]], axis=1))
        outs[3].append(sp[:, dec_seq - (CONV_W - 1):dec_seq, Q_OFF:Q_OFF + 2 * MLSTM_WIDTH])
        outs[5].append(c); outs[7].append(n); outs[9].append(m)

    fw = final_norm_w[None, :]
    y_prompt = _final_norm(xp, fw, tm=1024).reshape(batch, seq, D_MODEL)
    y_sample = _final_norm(xs, fw, tm=ts).reshape(dec_batch, SAMPLE_ROWS, D_MODEL)[:, :dec_seq]
    return (y_prompt, y_sample) + tuple(jnp.stack(o) for o in outs)
```

```python
import functools

import jax
import jax.numpy as jnp
from jax import lax
from jax.experimental import pallas as pl
from jax.experimental.pallas import tpu as pltpu

F32 = jnp.float32
BF16 = jnp.bfloat16

D_MODEL = 1024
DEPTH = 4
POOL_WINDOWS = (2, 4, 8, 16)
POOL_WIDTH = D_MODEL
POOL_GROUP_DIM = POOL_WIDTH // len(POOL_WINDOWS)
POOL_HIST = max(POOL_WINDOWS) - 1
MLSTM_WIDTH = 2 * D_MODEL
N_HEADS = 4
HEAD_DIM = MLSTM_WIDTH // N_HEADS
CONV_W = 4
PLE_DIM = 256
CHUNK = 128
EPS = 1e-6
PAST_LEN = 16384

IF_OFF = 2 * POOL_WIDTH + 5 * MLSTM_WIDTH
N_GATES = 2 * N_HEADS
W_MAIN_COLS = IF_OFF
W_TAIL_COLS = 2 * D_MODEL

PA_Q, PA_K, PA_POOL_IN = 0, MLSTM_WIDTH, 2 * MLSTM_WIDTH
PA_COLS = 2 * MLSTM_WIDTH + POOL_WIDTH
PB_V, PB_O, PB_MZ = 0, MLSTM_WIDTH, 2 * MLSTM_WIDTH
PB_GATE_A = 3 * MLSTM_WIDTH
PB_GATE_B = PB_GATE_A + D_MODEL
PB_POOL_Z = PB_GATE_B + D_MODEL
PB_COLS = PB_POOL_Z + POOL_WIDTH

LANES = 128
SUBLANES = 8
POOL_HALO = 16
CONV_HALO = SUBLANES
SAMPLE_ROWS = SUBLANES
NORM_ROWS = 256
VMEM_LIMIT = 56 * 1024 * 1024


def _sigmoid(x):
    return 1.0 / (1.0 + jnp.exp(-x))


def _log_sigmoid(x):
    return jnp.minimum(x, 0.0) - jnp.log1p(jnp.exp(-jnp.abs(x)))


def _rms(x, w):
    return x * lax.rsqrt(jnp.mean(x * x, axis=-1, keepdims=True) + EPS) * w


def _dot(a, b):
    return jnp.dot(a, b, preferred_element_type=F32)


def _params(*sem):
    return pltpu.CompilerParams(dimension_semantics=sem, vmem_limit_bytes=VMEM_LIMIT)


def _inproj_tiles(tn):
    u = D_MODEL // tn
    n_main = W_MAIN_COLS // tn
    n_total = n_main + W_TAIL_COLS // tn

    def is_pa(n):
        return (n < u) | ((n >= 2 * u) & (n < 6 * u))

    def pa_idx(n):
        return jnp.where(n < u, PA_POOL_IN // tn + n,
                         jnp.where(n < 2 * u, PA_POOL_IN // tn + u - 1,
                                   jnp.where(n < 6 * u, n - 2 * u, 4 * u - 1)))

    def pb_idx(n):
        return jnp.where(n < u, PB_POOL_Z // tn,
                         jnp.where(n < 2 * u, PB_POOL_Z // tn + n - u,
                                   jnp.where(n < 6 * u, PB_POOL_Z // tn + u - 1, n - 6 * u)))

    return n_main, n_total, is_pa, pa_idx, pb_idx


def _inproj_kernel(x_ref, nw_ref, w_ref, wt_ref, wih_ref, wil_ref, bif_ref, pa_ref, pb_ref, gates_ref,
                   h_ref, *, tn):
    n_main, _, is_pa, _, _ = _inproj_tiles(tn)
    n = pl.program_id(1)

    @pl.when(n == 0)
    def _():
        def norm_rows(r, carry):
            rs = pl.ds(pl.multiple_of(r * NORM_ROWS, NORM_ROWS), NORM_ROWS)
            h = _rms(x_ref[rs, :], nw_ref[...])
            hh = h.astype(BF16)
            hl = (h - hh.astype(F32)).astype(BF16)
            h_ref[rs, :] = hh
            gates_ref[rs, :] = (_dot(hh, wih_ref[...]) + _dot(hl, wih_ref[...])
                                + _dot(hh, wil_ref[...]) + bif_ref[...])
            return carry

        lax.fori_loop(0, x_ref.shape[0] // NORM_ROWS, norm_rows, 0)

    @pl.when(is_pa(n))
    def _():
        pa_ref[...] = _dot(h_ref[...], w_ref[...])

    @pl.when(jnp.logical_not(is_pa(n)) & (n < n_main))
    def _():
        pb_ref[...] = _dot(h_ref[...], w_ref[...]).astype(pb_ref.dtype)

    @pl.when(n >= n_main)
    def _():
        pb_ref[...] = _dot(h_ref[...], wt_ref[...]).astype(pb_ref.dtype)


def _inproj(x, nw, w_main, w_tail, wih, wil, bif, *, l, tm, tn, pb_dtype):
    t = x.shape[0]
    n_main, n_total, _, pa_idx, pb_idx = _inproj_tiles(tn)
    return pl.pallas_call(
        functools.partial(_inproj_kernel, tn=tn),
        grid=(t // tm, n_total),
        in_specs=[
            pl.BlockSpec((tm, D_MODEL), lambda m, n: (m, 0)),
            pl.BlockSpec((None, 1, D_MODEL), lambda m, n: (l, 0, 0)),
            pl.BlockSpec((None, D_MODEL, tn), lambda m, n: (l, 0, jnp.minimum(n, n_main - 1))),
            pl.BlockSpec((None, D_MODEL, tn), lambda m, n: (l, 0, jnp.maximum(n - n_main, 0))),
            pl.BlockSpec((None, D_MODEL, LANES), lambda m, n: (l, 0, 0)),
            pl.BlockSpec((None, D_MODEL, LANES), lambda m, n: (l, 0, 0)),
            pl.BlockSpec((None, 1, LANES), lambda m, n: (l, 0, 0)),
        ],
        out_specs=[
            pl.BlockSpec((tm, tn), lambda m, n: (m, pa_idx(n))),
            pl.BlockSpec((tm, tn), lambda m, n: (m, pb_idx(n))),
            pl.BlockSpec((tm, LANES), lambda m, n: (m, 0)),
        ],
        out_shape=[
            jax.ShapeDtypeStruct((t, PA_COLS), F32),
            jax.ShapeDtypeStruct((t, PB_COLS), pb_dtype),
            jax.ShapeDtypeStruct((t, LANES), F32),
        ],
        scratch_shapes=[pltpu.VMEM((tm, D_MODEL), BF16)],
        compiler_params=_params("parallel", "arbitrary"),
        name="inproj",
    )(x, nw, w_main, w_tail, wih, wil, bif)


def _pool_kernel(*refs, nb, rows, seq_tiles, pos0):
    if seq_tiles > 1:
        u_ref, z_ref, halo_ref, hist_ref, pw_ref, ps_ref, wd_ref, a_ref, xs_ref = refs
        st = pl.program_id(0) % seq_tiles
        halo = jnp.where(st == 0, hist_ref[...], halo_ref[...].reshape(1, POOL_HALO, POOL_WIDTH))
        tile_pos0 = pos0 + st * rows
    else:
        u_ref, z_ref, hist_ref, pw_ref, ps_ref, wd_ref, a_ref, xs_ref = refs
        halo = hist_ref[...]
        tile_pos0 = pos0
    xs_ref[:, 0:POOL_HALO, :] = halo
    xs_ref[:, POOL_HALO:, :] = u_ref[...].reshape(nb, rows, POOL_WIDTH)
    pos = tile_pos0 + lax.broadcasted_iota(jnp.int32, (nb, rows, POOL_GROUP_DIM), 1)
    mixed = []
    for g, win in enumerate(POOL_WINDOWS):
        cols = slice(g * POOL_GROUP_DIM, (g + 1) * POOL_GROUP_DIM)
        cur = xs_ref[:, POOL_HALO:POOL_HALO + rows, cols]
        acc = cur
        for j in range(1, win):
            acc = acc + xs_ref[:, POOL_HALO - j:POOL_HALO - j + rows, cols]
        cnt = jnp.minimum(pos + 1, win).astype(F32)
        pooled = (acc / cnt - cur).reshape(nb * rows, POOL_GROUP_DIM)
        mixed.append(_dot(pooled.astype(BF16), pw_ref[g]))
    mixed = jnp.concatenate(mixed, axis=-1) * ps_ref[...]
    z = z_ref[...].astype(F32)
    act = mixed * (z * _sigmoid(z))
    a_ref[...] = _dot(act.astype(BF16), wd_ref[...])


def _pool(pa, pb, hist, pw, ps, wd, *, l, hist_l, nb, rows, seq_tiles, pos0):
    t = pa.shape[0]
    tm = nb * rows
    in_specs = [
        pl.BlockSpec((tm, POOL_WIDTH), lambda i: (i, PA_POOL_IN // POOL_WIDTH)),
        pl.BlockSpec((tm, POOL_WIDTH), lambda i: (i, PB_POOL_Z // POOL_WIDTH)),
    ]
    args = [pa, pb]
    if seq_tiles > 1:
        per = rows // POOL_HALO
        in_specs.append(pl.BlockSpec((POOL_HALO, POOL_WIDTH),
                                     lambda i: (jnp.maximum(i * per - 1, 0), PA_POOL_IN // POOL_WIDTH)))
        args.append(pa)
        in_specs.append(pl.BlockSpec((None, 1, POOL_HALO, POOL_WIDTH),
                                     lambda i: (hist_l, i // seq_tiles, 0, 0)))
    else:
        in_specs.append(pl.BlockSpec((None, nb, POOL_HALO, POOL_WIDTH), lambda i: (hist_l, i, 0, 0)))
    args.append(hist)
    in_specs += [
        pl.BlockSpec((None, len(POOL_WINDOWS), POOL_GROUP_DIM, POOL_GROUP_DIM), lambda i: (l, 0, 0, 0)),
        pl.BlockSpec((None, 1, POOL_WIDTH), lambda i: (l, 0, 0)),
        pl.BlockSpec((None, POOL_WIDTH, D_MODEL), lambda i: (l, 0, 0)),
    ]
    args += [pw, ps, wd]
    return pl.pallas_call(
        functools.partial(_pool_kernel, nb=nb, rows=rows, seq_tiles=seq_tiles, pos0=pos0),
        grid=(t // tm,),
        in_specs=in_specs,
        out_specs=pl.BlockSpec((tm, D_MODEL), lambda i: (i, 0)),
        out_shape=jax.ShapeDtypeStruct((t, D_MODEL), F32),
        scratch_shapes=[pltpu.VMEM((nb, POOL_HALO + rows, POOL_WIDTH), F32)],
        compiler_params=_params("parallel"),
        name="pool",
    )(*args)


def _conv_silu(xs_ref, w_ref, b_ref, rows, cols):
    out = b_ref[:, cols] + xs_ref[CONV_HALO:CONV_HALO + rows, cols] * w_ref[CONV_W - 1:CONV_W, cols]
    for j in range(CONV_W - 1):
        start = CONV_HALO - (CONV_W - 1) + j
        out = out + xs_ref[start:start + rows, cols] * w_ref[j:j + 1, cols]
    return out * _sigmoid(out)


def _mlstm_kernel(*refs, rows, n_valid, nc, zero_init, n_alias):
    refs = list(refs)
    q_ref, k_ref, v_ref, o_ref, mz_ref = refs[:5]
    refs = refs[5:]
    if nc > 1:
        qh_ref, kh_ref = refs[:2]
        refs = refs[2:]
    hist_ref, g_ref, cw_ref, cb_ref, nw_ref = refs[:5]
    refs = refs[5:]
    if not zero_init:
        c0_ref, n0_ref, m0_ref = refs[:3]
        refs = refs[3:]
    refs = refs[n_alias:]
    hcg_ref, c_ref, n_ref, m_ref, xq_ref, xk_ref = refs

    ci = pl.program_id(1)

    if zero_init:
        @pl.when(ci == 0)
        def _():
            c_ref[...] = jnp.zeros(c_ref.shape, F32)
            n_ref[...] = jnp.zeros(n_ref.shape, F32)
            m_ref[...] = jnp.zeros(m_ref.shape, F32)

    kq = slice(0, MLSTM_WIDTH)
    kk = slice(MLSTM_WIDTH, 2 * MLSTM_WIDTH)
    if nc > 1:
        first = ci == 0
        xq_ref[0:CONV_HALO, :] = jnp.where(first, hist_ref[0, :, kq], qh_ref[...])
        xk_ref[0:CONV_HALO, :] = jnp.where(first, hist_ref[0, :, kk], kh_ref[...])
    else:
        xq_ref[0:CONV_HALO, :] = hist_ref[0, :, kq]
        xk_ref[0:CONV_HALO, :] = hist_ref[0, :, kk]
    xq_ref[CONV_HALO:, :] = q_ref[...]
    xk_ref[CONV_HALO:, :] = k_ref[...]

    row = lax.broadcasted_iota(jnp.int32, (rows, rows), 0)
    col = lax.broadcasted_iota(jnp.int32, (rows, rows), 1)
    causal = col <= row
    diag = col == row
    gates = g_ref[...]

    for h in range(N_HEADS):
        hs = slice(h * HEAD_DIM, (h + 1) * HEAD_DIM)
        ks = slice(MLSTM_WIDTH + h * HEAD_DIM, MLSTM_WIDTH + (h + 1) * HEAD_DIM)
        q = _conv_silu(xq_ref, cw_ref, cb_ref, rows, hs)
        k = _conv_silu_k(xk_ref, cw_ref, cb_ref, rows, hs, ks) * (HEAD_DIM ** -0.5)
        v = v_ref[:, hs].astype(F32)

        i_col = gates[:, h:h + 1]
        ls_col = _log_sigmoid(gates[:, N_HEADS + h:N_HEADS + h + 1])
        i_row = jnp.sum(jnp.where(diag, i_col, 0.0), axis=0, keepdims=True)
        ls_row = jnp.sum(jnp.where(diag, ls_col, 0.0), axis=0, keepdims=True)
        b_col = jnp.sum(jnp.where(causal, ls_row, 0.0), axis=1, keepdims=True)
        b_row = jnp.sum(jnp.where(row <= col, ls_col, 0.0), axis=0, keepdims=True)

        if zero_init:
            c, n, m = c_ref[0, h], n_ref[0, h], m_ref[0, h][:, 0:1]
        else:
            c, n, m = c0_ref[0, h], n0_ref[0, h], m0_ref[0, h]

        log_d = jnp.where(causal, b_col - b_row + i_row, -jnp.inf)
        m_inter = b_col + m
        m_t = jnp.maximum(m_inter, jnp.max(log_d, axis=-1, keepdims=True))
        d = jnp.exp(log_d - m_t)
        qb, kb, vb = q.astype(BF16), k.astype(BF16), v.astype(BF16)
        s = lax.dot_general(qb, kb, (((1,), (1,)), ((), ())), preferred_element_type=F32) * d
        g = jnp.exp(m_inter - m_t)
        num = _dot(s.astype(BF16), vb) + g * _dot(qb, c.astype(BF16))
        den = jnp.sum(s, axis=-1, keepdims=True) + g * jnp.sum(q * n, axis=-1, keepdims=True)
        hh = num / jnp.maximum(jnp.abs(den), jnp.exp(-m_t))

        b_last = b_col[n_valid - 1:n_valid, :]
        m_new = m_t[n_valid - 1:n_valid, :]
        decay = jnp.exp(b_last + m - m_new)
        w_s = jnp.exp(b_last - b_col + i_col - m_new)
        kw = k * w_s
        if n_valid < rows:
            tok = lax.broadcasted_iota(jnp.int32, (rows, HEAD_DIM), 0)
            kw = jnp.where(tok < n_valid, kw, 0.0)
        c_ref[0, h] = decay * c + lax.dot_general(kw.astype(BF16), vb, (((0,), (0,)), ((), ())),
                                                  preferred_element_type=F32)
        n_ref[0, h] = decay * n + jnp.sum(kw, axis=0, keepdims=True)
        m_ref[0, h] = jnp.broadcast_to(m_new, (1, LANES))

        hc = hh * _sigmoid(o_ref[:, hs].astype(F32))
        y = _rms(hc, nw_ref[:, hs])
        mz = mz_ref[:, hs].astype(F32)
        hcg_ref[:, hs] = (y * (mz * _sigmoid(mz))).astype(hcg_ref.dtype)


def _conv_silu_k(xs_ref, w_ref, b_ref, rows, hs, ks):
    out = b_ref[:, ks] + xs_ref[CONV_HALO:CONV_HALO + rows, hs] * w_ref[CONV_W - 1:CONV_W, ks]
    for j in range(CONV_W - 1):
        start = CONV_HALO - (CONV_W - 1) + j
        out = out + xs_ref[start:start + rows, hs] * w_ref[j:j + 1, ks]
    return out * _sigmoid(out)


def _mlstm(pa, pb, gates, hist8, cw, cb, nw, state, prev, *, l, hist_l, nseq, rows, n_valid, nc,
           hcg_dtype):
    t = pa.shape[0]
    zero_init = state is None
    mw, hb = MLSTM_WIDTH, HEAD_DIM

    def tok(off):
        return pl.BlockSpec((rows, mw), lambda b, c: (b * nc + c, off // mw))

    in_specs = [tok(PA_Q), tok(PA_K), tok(PB_V), tok(PB_O), tok(PB_MZ)]
    args = [pa, pa, pb, pb, pb]
    if nc > 1:
        per = rows // CONV_HALO

        def halo(off):
            return pl.BlockSpec((CONV_HALO, mw),
                                lambda b, c: (jnp.maximum((b * nc + c) * per - 1, 0), off // mw))

        in_specs += [halo(PA_Q), halo(PA_K)]
        args += [pa, pa]
    in_specs += [
        pl.BlockSpec((None, 1, CONV_HALO, 2 * mw), lambda b, c: (hist_l, b, 0, 0)),
        pl.BlockSpec((rows, LANES), lambda b, c: (b * nc + c, 0)),
        pl.BlockSpec((None, CONV_W, 2 * mw), lambda b, c: (l, 0, 0)),
        pl.BlockSpec((None, 1, 2 * mw), lambda b, c: (l, 0, 0)),
        pl.BlockSpec((None, 1, mw), lambda b, c: (l, 0, 0)),
    ]
    args += [hist8, gates, cw, cb, nw]
    c_spec = pl.BlockSpec((None, 1, N_HEADS, hb, hb), lambda b, c: (l, b, 0, 0, 0))
    n_spec = pl.BlockSpec((None, 1, N_HEADS, 1, hb), lambda b, c: (l, b, 0, 0, 0))
    if not zero_init:
        in_specs += [c_spec, n_spec,
                     pl.BlockSpec((None, 1, N_HEADS, 1, 1), lambda b, c: (l, b, 0, 0, 0))]
        args += list(state)
    aliases = {}
    if prev is not None:
        for j, p in enumerate(prev):
            aliases[len(args)] = 1 + j
            in_specs.append(pl.BlockSpec(memory_space=pl.ANY))
            args.append(p)
    n_alias = len(aliases)
    return pl.pallas_call(
        functools.partial(_mlstm_kernel, rows=rows, n_valid=n_valid, nc=nc, zero_init=zero_init,
                          n_alias=n_alias),
        grid=(nseq, nc),
        in_specs=in_specs,
        out_specs=[
            pl.BlockSpec((rows, mw), lambda b, c: (b * nc + c, 0)),
            c_spec,
            n_spec,
            pl.BlockSpec((None, 1, N_HEADS, 1, LANES), lambda b, c: (l, b, 0, 0, 0)),
        ],
        out_shape=[
            jax.ShapeDtypeStruct((t, mw), hcg_dtype),
            jax.ShapeDtypeStruct((DEPTH, nseq, N_HEADS, hb, hb), F32),
            jax.ShapeDtypeStruct((DEPTH, nseq, N_HEADS, 1, hb), F32),
            jax.ShapeDtypeStruct((DEPTH, nseq, N_HEADS, 1, LANES), F32),
        ],
        scratch_shapes=[pltpu.VMEM((CONV_HALO + rows, mw), F32),
                        pltpu.VMEM((CONV_HALO + rows, mw), F32)],
        input_output_aliases=aliases,
        compiler_params=_params("parallel", "arbitrary"),
        name="mlstm",
    )(*args)


def _merge_kernel(x_ref, a_ref, hcg_ref, ga_ref, gb_ref, pe_ref, wmd_ref, wo_ref, wple_ref, pnw_ref,
                  wpg_ref, out_ref):
    bm = _dot(hcg_ref[...].astype(BF16), wmd_ref[...])
    mix = (_sigmoid(ga_ref[...].astype(F32)) * a_ref[...]
           + _sigmoid(gb_ref[...].astype(F32)) * bm)
    x1 = x_ref[...] + _dot(mix.astype(BF16), wo_ref[...])
    r = _rms(x1, pnw_ref[...])
    gate = _sigmoid(_dot(r.astype(BF16), wpg_ref[...]))
    out_ref[...] = x1 + _dot(pe_ref[...].astype(BF16), wple_ref[...]) * gate


def _merge(x, a, hcg, pb, pe, wmd, wo, wple, pnw, wpg, *, l, tm):
    t = x.shape[0]

    def lw(*shape):
        return pl.BlockSpec((None,) + shape, lambda i: (l, 0, 0))

    return pl.pallas_call(
        _merge_kernel,
        grid=(t // tm,),
        in_specs=[
            pl.BlockSpec((tm, D_MODEL), lambda i: (i, 0)),
            pl.BlockSpec((tm, D_MODEL), lambda i: (i, 0)),
            pl.BlockSpec((tm, MLSTM_WIDTH), lambda i: (i, 0)),
            pl.BlockSpec((tm, D_MODEL), lambda i: (i, PB_GATE_A // D_MODEL)),
            pl.BlockSpec((tm, D_MODEL), lambda i: (i, PB_GATE_B // D_MODEL)),
            pl.BlockSpec((None, tm, PLE_DIM), lambda i: (l, i, 0)),
            lw(MLSTM_WIDTH, D_MODEL),
            lw(D_MODEL, D_MODEL),
            lw(PLE_DIM, D_MODEL),
            lw(1, D_MODEL),
            lw(D_MODEL, D_MODEL),
        ],
        out_specs=pl.BlockSpec((tm, D_MODEL), lambda i: (i, 0)),
        out_shape=jax.ShapeDtypeStruct((t, D_MODEL), F32),
        compiler_params=_params("parallel"),
        name="merge",
    )(x, a, hcg, pb, pb, pe, wmd, wo, wple, pnw, wpg)


def _final_kernel(x_ref, w_ref, y_ref):
    y_ref[...] = _rms(x_ref[...], w_ref[...])


def _final_norm(x, w, *, tm):
    t = x.shape[0]
    return pl.pallas_call(
        _final_kernel,
        grid=(t // tm,),
        in_specs=[pl.BlockSpec((tm, D_MODEL), lambda i: (i, 0)),
                  pl.BlockSpec((1, D_MODEL), lambda i: (0, 0))],
        out_specs=pl.BlockSpec((tm, D_MODEL), lambda i: (i, 0)),
        out_shape=jax.ShapeDtypeStruct((t, D_MODEL), F32),
        compiler_params=_params("parallel"),
        name="final_norm",
    )(x, w)


def kernel(x_prompt, x_sample, state_pool, state_conv, state_mlstm_C, state_mlstm_n, state_mlstm_m,
           p_prompt, p_sample, norm_w, w_in, b_if, conv_w, conv_b, pool_w, pool_scale, w_pool_down,
           mlstm_norm_w, w_mlstm_down, w_out, w_ple, ple_norm_w, w_ple_gate, final_norm_w):
    batch, seq, _ = x_prompt.shape
    dec_batch, dec_seq, _ = x_sample.shape
    assert seq % CHUNK == 0 and CONV_W - 1 <= dec_seq <= SAMPLE_ROWS
    pad_rows = SAMPLE_ROWS - dec_seq
    tp = batch * seq
    ts = dec_batch * SAMPLE_ROWS

    w_bf = w_in.astype(BF16)
    w_tail = w_bf[:, :, IF_OFF + N_GATES:]
    w_if = jnp.pad(w_in[:, :, IF_OFF:IF_OFF + N_GATES], ((0, 0), (0, 0), (0, LANES - N_GATES)))
    w_if_hi = w_if.astype(BF16)
    w_if_lo = (w_if - w_if_hi.astype(F32)).astype(BF16)
    b_if_p = jnp.pad(b_if, ((0, 0), (0, LANES - N_GATES)))[:, None, :]
    pool_w_b = pool_w.astype(BF16)
    w_pool_down_b = w_pool_down.astype(BF16)
    w_mlstm_down_b = w_mlstm_down.astype(BF16)
    w_out_b = w_out.astype(BF16)
    w_ple_b = w_ple.astype(BF16)
    w_ple_gate_b = w_ple_gate.astype(BF16)
    norm_w3, conv_b3, pool_scale3 = norm_w[:, None, :], conv_b[:, None, :], pool_scale[:, None, :]
    mlstm_norm_w3, ple_norm_w3 = mlstm_norm_w[:, None, :], ple_norm_w[:, None, :]

    xp = x_prompt.reshape(tp, D_MODEL)
    xs = jnp.pad(x_sample, ((0, 0), (0, pad_rows), (0, 0))).reshape(ts, D_MODEL)
    pe_s = jnp.pad(p_sample, ((0, 0), (0, 0), (0, pad_rows), (0, 0))).reshape(DEPTH, ts, PLE_DIM)
    pe_p = p_prompt.reshape(DEPTH, tp, PLE_DIM)
    hist16_s = jnp.pad(state_pool, ((0, 0), (0, 0), (POOL_HALO - POOL_HIST, 0), (0, 0)))
    hist8_s = jnp.pad(state_conv, ((0, 0), (0, 0), (CONV_HALO - (CONV_W - 1), 0), (0, 0)))
    hist16_p = jnp.zeros((1, batch, POOL_HALO, POOL_WIDTH), F32)
    hist8_p = jnp.zeros((1, batch, CONV_HALO, 2 * MLSTM_WIDTH), F32)
    state_s = (state_mlstm_C, state_mlstm_n[:, :, :, None, :], state_mlstm_m[:, :, :, None, None])

    pool_tile = 512
    pool_p, pool_s, conv_p, conv_s = [], [], [], []
    st_p = st_s = None
    for l in range(DEPTH):
        def layer(x, pe, hist16, hist8, state, prev, *, hist_l, nseq, rows, n_valid, nc, tm_in, tn, nb,
                  prow, seq_tiles, pos0, tm_merge, act_dtype):
            pa, pb, gates = _inproj(x, norm_w3, w_bf, w_tail, w_if_hi, w_if_lo, b_if_p, l=l, tm=tm_in,
                                    tn=tn, pb_dtype=act_dtype)
            a = _pool(pa, pb, hist16, pool_w_b, pool_scale3, w_pool_down_b, l=l, hist_l=hist_l, nb=nb,
                      rows=prow, seq_tiles=seq_tiles, pos0=pos0)
            hcg, c, n, m = _mlstm(pa, pb, gates, hist8, conv_w, conv_b3, mlstm_norm_w3, state, prev,
                                  l=l, hist_l=hist_l, nseq=nseq, rows=rows, n_valid=n_valid, nc=nc,
                                  hcg_dtype=act_dtype)
            x = _merge(x, a, hcg, pb, pe, w_mlstm_down_b, w_out_b, w_ple_b, ple_norm_w3, w_ple_gate_b,
                       l=l, tm=tm_merge)
            return x, pa, (c, n, m)

        xp, pa_p, st_p = layer(xp, pe_p, hist16_p, hist8_p, None, st_p, hist_l=0, nseq=batch,
                               rows=CHUNK, n_valid=CHUNK, nc=seq // CHUNK, tm_in=2048, tn=512, nb=1,
                               prow=pool_tile, seq_tiles=seq // pool_tile, pos0=0, tm_merge=256,
                               act_dtype=BF16)
        pp = pa_p.reshape(batch, seq, PA_COLS)
        pool_p.append(pp[:, seq - POOL_HIST:, PA_POOL_IN:])
        conv_p.append(pp[:, seq - (CONV_W - 1):, :PA_POOL_IN])

        xs, pa_s, st_s = layer(xs, pe_s, hist16_s, hist8_s, state_s, st_s, hist_l=l, nseq=dec_batch,
                               rows=SAMPLE_ROWS, n_valid=dec_seq, nc=1, tm_in=ts, tn=512, nb=16,
                               prow=SAMPLE_ROWS, seq_tiles=1, pos0=PAST_LEN, tm_merge=256,
                               act_dtype=F32)
        sp = pa_s.reshape(dec_batch, SAMPLE_ROWS, PA_COLS)
        pool_s.append(jnp.concatenate([state_pool[l][:, dec_seq:], sp[:, :dec_seq, PA_POOL_IN:]], axis=1))
        conv_s.append(sp[:, dec_seq - (CONV_W - 1):dec_seq, :PA_POOL_IN])

    fw = final_norm_w[None, :]
    y_prompt = _final_norm(xp, fw, tm=1024).reshape(batch, seq, D_MODEL)
    y_sample = _final_norm(xs, fw, tm=ts).reshape(dec_batch, SAMPLE_ROWS, D_MODEL)[:, :dec_seq]
    c_p, n_p, m_p = st_p
    c_s, n_s, m_s = st_s
    return (y_prompt, y_sample, jnp.stack(pool_p), jnp.stack(pool_s), jnp.stack(conv_p),
            jnp.stack(conv_s), c_p, c_s, n_p[:, :, :, 0, :], n_s[:, :, :, 0, :],
            m_p[:, :, :, 0, 0], m_s[:, :, :, 0, 0])
```

```python
import functools

import jax
import jax.numpy as jnp
from jax import lax
from jax.experimental import pallas as pl
from jax.experimental.pallas import tpu as pltpu

F32 = jnp.float32
BF16 = jnp.bfloat16

D_MODEL = 1024
DEPTH = 4
POOL_WINDOWS = (2, 4, 8, 16)
POOL_WIDTH = D_MODEL
POOL_GROUP_DIM = POOL_WIDTH // len(POOL_WINDOWS)
POOL_HIST = max(POOL_WINDOWS) - 1
MLSTM_WIDTH = 2 * D_MODEL
N_HEADS = 4
HEAD_DIM = MLSTM_WIDTH // N_HEADS
CONV_W = 4
PLE_DIM = 256
CHUNK = 128
EPS = 1e-6
PAST_LEN = 16384

IF_OFF = 2 * POOL_WIDTH + 5 * MLSTM_WIDTH
N_GATES = 2 * N_HEADS
W_MAIN_COLS = IF_OFF
W_TAIL_COLS = 2 * D_MODEL

PA_Q, PA_K, PA_POOL_IN = 0, MLSTM_WIDTH, 2 * MLSTM_WIDTH
PA_COLS = 2 * MLSTM_WIDTH + POOL_WIDTH
PB_V, PB_O, PB_MZ = 0, MLSTM_WIDTH, 2 * MLSTM_WIDTH
PB_GATE_A = 3 * MLSTM_WIDTH
PB_GATE_B = PB_GATE_A + D_MODEL
PB_POOL_Z = PB_GATE_B + D_MODEL
PB_COLS = PB_POOL_Z + POOL_WIDTH

LANES = 128
SUBLANES = 8
POOL_HALO = 16
CONV_HALO = SUBLANES
SAMPLE_ROWS = SUBLANES
NORM_ROWS = 256
VMEM_LIMIT = 56 * 1024 * 1024


def _sigmoid(x):
    return 0.5 * jnp.tanh(0.5 * x) + 0.5


def _silu(x):
    hx = 0.5 * x
    return hx * jnp.tanh(hx) + hx


def _log_sigmoid(x):
    return jnp.minimum(x, 0.0) - jnp.log1p(jnp.exp(-jnp.abs(x)))


def _rms(x, w):
    return x * lax.rsqrt(jnp.mean(x * x, axis=-1, keepdims=True) + EPS) * w


def _dot(a, b):
    return jnp.dot(a, b, preferred_element_type=F32)


def _params(*sem):
    return pltpu.CompilerParams(dimension_semantics=sem, vmem_limit_bytes=VMEM_LIMIT)


def _inproj_tiles(tn):
    u = D_MODEL // tn
    n_main = W_MAIN_COLS // tn
    n_total = n_main + W_TAIL_COLS // tn

    def is_pa(n):
        return (n < u) | ((n >= 2 * u) & (n < 6 * u))

    def pa_idx(n):
        return jnp.where(n < u, PA_POOL_IN // tn + n,
                         jnp.where(n < 2 * u, PA_POOL_IN // tn + u - 1,
                                   jnp.where(n < 6 * u, n - 2 * u, 4 * u - 1)))

    def pb_idx(n):
        return jnp.where(n < u, PB_POOL_Z // tn,
                         jnp.where(n < 2 * u, PB_POOL_Z // tn + n - u,
                                   jnp.where(n < 6 * u, PB_POOL_Z // tn + u - 1, n - 6 * u)))

    return n_main, n_total, is_pa, pa_idx, pb_idx


def _inproj_kernel(x_ref, nw_ref, w_ref, wt_ref, wih_ref, wil_ref, bif_ref, pa_ref, pb_ref, gates_ref,
                   h_ref, *, tn):
    n_main, _, is_pa, _, _ = _inproj_tiles(tn)
    n = pl.program_id(1)

    @pl.when(n == 0)
    def _():
        def norm_rows(r, carry):
            rs = pl.ds(pl.multiple_of(r * NORM_ROWS, NORM_ROWS), NORM_ROWS)
            h = _rms(x_ref[rs, :], nw_ref[...])
            hh = h.astype(BF16)
            hl = (h - hh.astype(F32)).astype(BF16)
            h_ref[rs, :] = hh
            gates_ref[rs, :] = (_dot(hh, wih_ref[...]) + _dot(hl, wih_ref[...])
                                + _dot(hh, wil_ref[...]) + bif_ref[...])
            return carry

        lax.fori_loop(0, x_ref.shape[0] // NORM_ROWS, norm_rows, 0)

    @pl.when(is_pa(n))
    def _():
        pa_ref[...] = _dot(h_ref[...], w_ref[...].astype(BF16))

    @pl.when(jnp.logical_not(is_pa(n)) & (n < n_main))
    def _():
        pb_ref[...] = _dot(h_ref[...], w_ref[...].astype(BF16)).astype(pb_ref.dtype)

    @pl.when(n >= n_main)
    def _():
        pb_ref[...] = _dot(h_ref[...], wt_ref[...].astype(BF16)).astype(pb_ref.dtype)


def _inproj(x, nw, w_main, w_tail, wih, wil, bif, *, l, tm, tn, pb_dtype):
    t = x.shape[0]
    n_main, n_total, _, pa_idx, pb_idx = _inproj_tiles(tn)
    return pl.pallas_call(
        functools.partial(_inproj_kernel, tn=tn),
        grid=(t // tm, n_total),
        in_specs=[
            pl.BlockSpec((tm, D_MODEL), lambda m, n: (m, 0)),
            pl.BlockSpec((None, 1, D_MODEL), lambda m, n: (l, 0, 0)),
            pl.BlockSpec((None, D_MODEL, tn), lambda m, n: (l, 0, jnp.minimum(n, n_main - 1))),
            pl.BlockSpec((None, D_MODEL, tn), lambda m, n: (l, 0, jnp.maximum(n - n_main, 0))),
            pl.BlockSpec((None, D_MODEL, LANES), lambda m, n: (l, 0, 0)),
            pl.BlockSpec((None, D_MODEL, LANES), lambda m, n: (l, 0, 0)),
            pl.BlockSpec((None, 1, LANES), lambda m, n: (l, 0, 0)),
        ],
        out_specs=[
            pl.BlockSpec((tm, tn), lambda m, n: (m, pa_idx(n))),
            pl.BlockSpec((tm, tn), lambda m, n: (m, pb_idx(n))),
            pl.BlockSpec((tm, LANES), lambda m, n: (m, 0)),
        ],
        out_shape=[
            jax.ShapeDtypeStruct((t, PA_COLS), F32),
            jax.ShapeDtypeStruct((t, PB_COLS), pb_dtype),
            jax.ShapeDtypeStruct((t, LANES), F32),
        ],
        scratch_shapes=[pltpu.VMEM((tm, D_MODEL), BF16)],
        compiler_params=_params("parallel", "arbitrary"),
        name="inproj",
    )(x, nw, w_main, w_tail, wih, wil, bif)


def _pool_kernel(*refs, nb, rows, seq_tiles, pos0):
    if seq_tiles > 1:
        u_ref, z_ref, halo_ref, hist_ref, pw_ref, ps_ref, wd_ref, a_ref, xs_ref = refs
        st = pl.program_id(0) % seq_tiles
        halo = jnp.where(st == 0, hist_ref[...], halo_ref[...].reshape(1, POOL_HALO, POOL_WIDTH))
        tile_pos0 = pos0 + st * rows
    else:
        u_ref, z_ref, hist_ref, pw_ref, ps_ref, wd_ref, a_ref, xs_ref = refs
        halo = hist_ref[...]
        tile_pos0 = pos0
    xs_ref[:, 0:POOL_HALO, :] = halo
    xs_ref[:, POOL_HALO:, :] = u_ref[...].reshape(nb, rows, POOL_WIDTH)
    ext = POOL_HALO + rows
    pos = tile_pos0 + lax.broadcasted_iota(jnp.int32, (nb, rows, POOL_GROUP_DIM), 1)
    mixed = []
    for g, win in enumerate(POOL_WINDOWS):
        cols = slice(g * POOL_GROUP_DIM, (g + 1) * POOL_GROUP_DIM)
        acc = xs_ref[:, :, cols].reshape(nb * ext, POOL_GROUP_DIM)
        span = 1
        while span < win:
            acc = acc + pltpu.roll(acc, span, axis=0)
            span *= 2
        acc = acc.reshape(nb, ext, POOL_GROUP_DIM)[:, POOL_HALO:, :]
        cnt = jnp.minimum(pos + 1, win).astype(F32)
        pooled = (acc / cnt - xs_ref[:, POOL_HALO:, cols]).reshape(nb * rows, POOL_GROUP_DIM)
        mixed.append(_dot(pooled.astype(BF16), pw_ref[g]))
    mixed = jnp.concatenate(mixed, axis=-1) * ps_ref[...]
    act = mixed * _silu(z_ref[...].astype(F32))
    a_ref[...] = _dot(act.astype(BF16), wd_ref[...])


def _pool(pa, pb, hist, pw, ps, wd, *, l, hist_l, nb, rows, seq_tiles, pos0):
    t = pa.shape[0]
    tm = nb * rows
    in_specs = [
        pl.BlockSpec((tm, POOL_WIDTH), lambda i: (i, PA_POOL_IN // POOL_WIDTH)),
        pl.BlockSpec((tm, POOL_WIDTH), lambda i: (i, PB_POOL_Z // POOL_WIDTH)),
    ]
    args = [pa, pb]
    if seq_tiles > 1:
        per = rows // POOL_HALO
        in_specs.append(pl.BlockSpec((POOL_HALO, POOL_WIDTH),
                                     lambda i: (jnp.maximum(i * per - 1, 0), PA_POOL_IN // POOL_WIDTH)))
        args.append(pa)
        in_specs.append(pl.BlockSpec((None, 1, POOL_HALO, POOL_WIDTH),
                                     lambda i: (hist_l, i // seq_tiles, 0, 0)))
    else:
        in_specs.append(pl.BlockSpec((None, nb, POOL_HALO, POOL_WIDTH), lambda i: (hist_l, i, 0, 0)))
    args.append(hist)
    in_specs += [
        pl.BlockSpec((None, len(POOL_WINDOWS), POOL_GROUP_DIM, POOL_GROUP_DIM), lambda i: (l, 0, 0, 0)),
        pl.BlockSpec((None, 1, POOL_WIDTH), lambda i: (l, 0, 0)),
        pl.BlockSpec((None, POOL_WIDTH, D_MODEL), lambda i: (l, 0, 0)),
    ]
    args += [pw, ps, wd]
    return pl.pallas_call(
        functools.partial(_pool_kernel, nb=nb, rows=rows, seq_tiles=seq_tiles, pos0=pos0),
        grid=(t // tm,),
        in_specs=in_specs,
        out_specs=pl.BlockSpec((tm, D_MODEL), lambda i: (i, 0)),
        out_shape=jax.ShapeDtypeStruct((t, D_MODEL), F32),
        scratch_shapes=[pltpu.VMEM((nb, POOL_HALO + rows, POOL_WIDTH), F32)],
        compiler_params=_params("parallel"),
        name="pool",
    )(*args)


def _conv_silu(halo, x, w_ref, b_ref, wcols):
    z = jnp.concatenate([halo, x], axis=0)
    acc = z * w_ref[0:1, wcols]
    for j in range(1, CONV_W):
        acc = pltpu.roll(acc, 1, axis=0) + z * w_ref[j:j + 1, wcols]
    return _silu(acc[CONV_HALO:, :] + b_ref[:, wcols])


def _mlstm_kernel(*refs, rows, n_valid, nc, zero_init, n_alias):
    refs = list(refs)
    q_ref, k_ref, v_ref, o_ref, mz_ref = refs[:5]
    refs = refs[5:]
    if nc > 1:
        qh_ref, kh_ref = refs[:2]
        refs = refs[2:]
    hist_ref, g_ref, cw_ref, cb_ref, nw_ref = refs[:5]
    refs = refs[5:]
    if not zero_init:
        c0_ref, n0_ref, m0_ref = refs[:3]
        refs = refs[3:]
    refs = refs[n_alias:]
    hcg_ref, c_ref, n_ref, m_ref = refs

    ci = pl.program_id(1)

    if zero_init:
        @pl.when(ci == 0)
        def _():
            c_ref[...] = jnp.zeros(c_ref.shape, F32)
            n_ref[...] = jnp.zeros(n_ref.shape, F32)
            m_ref[...] = jnp.zeros(m_ref.shape, F32)

    kq = slice(0, MLSTM_WIDTH)
    kk = slice(MLSTM_WIDTH, 2 * MLSTM_WIDTH)
    if nc > 1:
        first = ci == 0
        q_halo = jnp.where(first, hist_ref[0, :, kq], qh_ref[...])
        k_halo = jnp.where(first, hist_ref[0, :, kk], kh_ref[...])
    else:
        q_halo = hist_ref[0, :, kq]
        k_halo = hist_ref[0, :, kk]

    row = lax.broadcasted_iota(jnp.int32, (rows, rows), 0)
    col = lax.broadcasted_iota(jnp.int32, (rows, rows), 1)
    causal = col <= row
    diag = col == row
    gates = g_ref[...]

    for h in range(N_HEADS):
        hs = slice(h * HEAD_DIM, (h + 1) * HEAD_DIM)
        ks = slice(MLSTM_WIDTH + h * HEAD_DIM, MLSTM_WIDTH + (h + 1) * HEAD_DIM)
        q = _conv_silu(q_halo[:, hs], q_ref[:, hs], cw_ref, cb_ref, hs)
        k = _conv_silu(k_halo[:, hs], k_ref[:, hs], cw_ref, cb_ref, ks) * (HEAD_DIM ** -0.5)
        vb = v_ref[:, hs].astype(BF16)

        i_col = gates[:, h:h + 1]
        ls_col = _log_sigmoid(gates[:, N_HEADS + h:N_HEADS + h + 1])
        i_row = jnp.sum(jnp.where(diag, i_col, 0.0), axis=0, keepdims=True)
        ls_row = jnp.sum(jnp.where(diag, ls_col, 0.0), axis=0, keepdims=True)
        b_col = jnp.sum(jnp.where(causal, ls_row, 0.0), axis=1, keepdims=True)
        b_row = jnp.sum(jnp.where(row <= col, ls_col, 0.0), axis=0, keepdims=True)

        if zero_init:
            c, n, m = c_ref[0, h], n_ref[0, h], m_ref[0, h][:, 0:1]
        else:
            c, n, m = c0_ref[0, h], n0_ref[0, h], m0_ref[0, h]

        log_d = jnp.where(causal, b_col - b_row + i_row, -jnp.inf)
        m_inter = b_col + m
        m_t = jnp.maximum(m_inter, jnp.max(log_d, axis=-1, keepdims=True))
        d = jnp.exp(log_d - m_t)
        qb, kb = q.astype(BF16), k.astype(BF16)
        s = lax.dot_general(qb, kb, (((1,), (1,)), ((), ())), preferred_element_type=F32) * d
        g = jnp.exp(m_inter - m_t)
        num = _dot(s.astype(BF16), vb) + g * _dot(qb, c.astype(BF16))
        den = jnp.sum(s, axis=-1, keepdims=True) + g * jnp.sum(q * n, axis=-1, keepdims=True)
        hh = num / jnp.maximum(jnp.abs(den), jnp.exp(-m_t))

        b_last = b_col[n_valid - 1:n_valid, :]
        m_new = m_t[n_valid - 1:n_valid, :]
        decay = jnp.exp(b_last + m - m_new)
        w_s = jnp.exp(b_last - b_col + i_col - m_new)
        kw = k * w_s
        if n_valid < rows:
            tok = lax.broadcasted_iota(jnp.int32, (rows, HEAD_DIM), 0)
            kw = jnp.where(tok < n_valid, kw, 0.0)
        c_ref[0, h] = decay * c + lax.dot_general(kw.astype(BF16), vb, (((0,), (0,)), ((), ())),
                                                  preferred_element_type=F32)
        n_ref[0, h] = decay * n + jnp.sum(kw, axis=0, keepdims=True)
        m_ref[0, h] = jnp.broadcast_to(m_new, (1, LANES))

        hc = hh * _sigmoid(o_ref[:, hs].astype(F32))
        y = _rms(hc, nw_ref[:, hs])
        hcg_ref[:, hs] = (y * _silu(mz_ref[:, hs].astype(F32))).astype(hcg_ref.dtype)


def _mlstm(pa, pb, gates, hist8, cw, cb, nw, state, prev, *, l, hist_l, nseq, rows, n_valid, nc,
           hcg_dtype):
    t = pa.shape[0]
    zero_init = state is None
    mw, hb = MLSTM_WIDTH, HEAD_DIM

    def tok(off):
        return pl.BlockSpec((rows, mw), lambda b, c: (b * nc + c, off // mw))

    in_specs = [tok(PA_Q), tok(PA_K), tok(PB_V), tok(PB_O), tok(PB_MZ)]
    args = [pa, pa, pb, pb, pb]
    if nc > 1:
        per = rows // CONV_HALO

        def halo(off):
            return pl.BlockSpec((CONV_HALO, mw),
                                lambda b, c: (jnp.maximum((b * nc + c) * per - 1, 0), off // mw))

        in_specs += [halo(PA_Q), halo(PA_K)]
        args += [pa, pa]
    in_specs += [
        pl.BlockSpec((None, 1, CONV_HALO, 2 * mw), lambda b, c: (hist_l, b, 0, 0)),
        pl.BlockSpec((rows, LANES), lambda b, c: (b * nc + c, 0)),
        pl.BlockSpec((None, CONV_W, 2 * mw), lambda b, c: (l, 0, 0)),
        pl.BlockSpec((None, 1, 2 * mw), lambda b, c: (l, 0, 0)),
        pl.BlockSpec((None, 1, mw), lambda b, c: (l, 0, 0)),
    ]
    args += [hist8, gates, cw, cb, nw]
    c_spec = pl.BlockSpec((None, 1, N_HEADS, hb, hb), lambda b, c: (l, b, 0, 0, 0))
    n_spec = pl.BlockSpec((None, 1, N_HEADS, 1, hb), lambda b, c: (l, b, 0, 0, 0))
    if not zero_init:
        in_specs += [c_spec, n_spec,
                     pl.BlockSpec((None, 1, N_HEADS, 1, 1), lambda b, c: (l, b, 0, 0, 0))]
        args += list(state)
    aliases = {}
    if prev is not None:
        for j, p in enumerate(prev):
            aliases[len(args)] = 1 + j
            in_specs.append(pl.BlockSpec(memory_space=pl.ANY))
            args.append(p)
    n_alias = len(aliases)
    return pl.pallas_call(
        functools.partial(_mlstm_kernel, rows=rows, n_valid=n_valid, nc=nc, zero_init=zero_init,
                          n_alias=n_alias),
        grid=(nseq, nc),
        in_specs=in_specs,
        out_specs=[
            pl.BlockSpec((rows, mw), lambda b, c: (b * nc + c, 0)),
            c_spec,
            n_spec,
            pl.BlockSpec((None, 1, N_HEADS, 1, LANES), lambda b, c: (l, b, 0, 0, 0)),
        ],
        out_shape=[
            jax.ShapeDtypeStruct((t, mw), hcg_dtype),
            jax.ShapeDtypeStruct((DEPTH, nseq, N_HEADS, hb, hb), F32),
            jax.ShapeDtypeStruct((DEPTH, nseq, N_HEADS, 1, hb), F32),
            jax.ShapeDtypeStruct((DEPTH, nseq, N_HEADS, 1, LANES), F32),
        ],
        input_output_aliases=aliases,
        compiler_params=_params("parallel", "arbitrary"),
        name="mlstm",
    )(*args)


def _merge_kernel(x_ref, a_ref, hcg_ref, ga_ref, gb_ref, pe_ref, wmd_ref, wo_ref, wple_ref, pnw_ref,
                  wpg_ref, fw_ref, out_ref, *, final):
    bm = _dot(hcg_ref[...].astype(BF16), wmd_ref[...])
    mix = (_sigmoid(ga_ref[...].astype(F32)) * a_ref[...]
           + _sigmoid(gb_ref[...].astype(F32)) * bm)
    x1 = x_ref[...] + _dot(mix.astype(BF16), wo_ref[...])
    r = _rms(x1, pnw_ref[...])
    gate = _sigmoid(_dot(r.astype(BF16), wpg_ref[...]))
    x2 = x1 + _dot(pe_ref[...].astype(BF16), wple_ref[...]) * gate
    out_ref[...] = _rms(x2, fw_ref[...]) if final else x2


def _merge(x, a, hcg, pb, pe, wmd, wo, wple, pnw, wpg, fw, *, l, tm):
    t = x.shape[0]

    def lw(*shape):
        return pl.BlockSpec((None,) + shape, lambda i: (l, 0, 0))

    return pl.pallas_call(
        functools.partial(_merge_kernel, final=(l == DEPTH - 1)),
        grid=(t // tm,),
        in_specs=[
            pl.BlockSpec((tm, D_MODEL), lambda i: (i, 0)),
            pl.BlockSpec((tm, D_MODEL), lambda i: (i, 0)),
            pl.BlockSpec((tm, MLSTM_WIDTH), lambda i: (i, 0)),
            pl.BlockSpec((tm, D_MODEL), lambda i: (i, PB_GATE_A // D_MODEL)),
            pl.BlockSpec((tm, D_MODEL), lambda i: (i, PB_GATE_B // D_MODEL)),
            pl.BlockSpec((None, tm, PLE_DIM), lambda i: (l, i, 0)),
            lw(MLSTM_WIDTH, D_MODEL),
            lw(D_MODEL, D_MODEL),
            lw(PLE_DIM, D_MODEL),
            lw(1, D_MODEL),
            lw(D_MODEL, D_MODEL),
            pl.BlockSpec((1, D_MODEL), lambda i: (0, 0)),
        ],
        out_specs=pl.BlockSpec((tm, D_MODEL), lambda i: (i, 0)),
        out_shape=jax.ShapeDtypeStruct((t, D_MODEL), F32),
        compiler_params=_params("parallel"),
        name="merge",
    )(x, a, hcg, pb, pb, pe, wmd, wo, wple, pnw, wpg, fw)


def kernel(x_prompt, x_sample, state_pool, state_conv, state_mlstm_C, state_mlstm_n, state_mlstm_m,
           p_prompt, p_sample, norm_w, w_in, b_if, conv_w, conv_b, pool_w, pool_scale, w_pool_down,
           mlstm_norm_w, w_mlstm_down, w_out, w_ple, ple_norm_w, w_ple_gate, final_norm_w):
    batch, seq, _ = x_prompt.shape
    dec_batch, dec_seq, _ = x_sample.shape
    assert seq % CHUNK == 0 and CONV_W - 1 <= dec_seq <= SAMPLE_ROWS
    pad_rows = SAMPLE_ROWS - dec_seq
    tp = batch * seq
    ts = dec_batch * SAMPLE_ROWS

    w_tail = w_in[:, :, IF_OFF + N_GATES:]
    w_if = jnp.pad(w_in[:, :, IF_OFF:IF_OFF + N_GATES], ((0, 0), (0, 0), (0, LANES - N_GATES)))
    w_if_hi = w_if.astype(BF16)
    w_if_lo = (w_if - w_if_hi.astype(F32)).astype(BF16)
    b_if_p = jnp.pad(b_if, ((0, 0), (0, LANES - N_GATES)))[:, None, :]
    pool_w_b = pool_w.astype(BF16)
    w_pool_down_b = w_pool_down.astype(BF16)
    w_mlstm_down_b = w_mlstm_down.astype(BF16)
    w_out_b = w_out.astype(BF16)
    w_ple_b = w_ple.astype(BF16)
    w_ple_gate_b = w_ple_gate.astype(BF16)
    norm_w3, conv_b3, pool_scale3 = norm_w[:, None, :], conv_b[:, None, :], pool_scale[:, None, :]
    mlstm_norm_w3, ple_norm_w3 = mlstm_norm_w[:, None, :], ple_norm_w[:, None, :]

    xp = x_prompt.reshape(tp, D_MODEL)
    xs = jnp.pad(x_sample, ((0, 0), (0, pad_rows), (0, 0))).reshape(ts, D_MODEL)
    pe_s = jnp.pad(p_sample, ((0, 0), (0, 0), (0, pad_rows), (0, 0))).reshape(DEPTH, ts, PLE_DIM)
    pe_p = p_prompt.reshape(DEPTH, tp, PLE_DIM)
    hist16_s = jnp.pad(state_pool, ((0, 0), (0, 0), (POOL_HALO - POOL_HIST, 0), (0, 0)))
    hist8_s = jnp.pad(state_conv, ((0, 0), (0, 0), (CONV_HALO - (CONV_W - 1), 0), (0, 0)))
    hist16_p = jnp.zeros((1, batch, POOL_HALO, POOL_WIDTH), F32)
    hist8_p = jnp.zeros((1, batch, CONV_HALO, 2 * MLSTM_WIDTH), F32)
    state_s = (state_mlstm_C, state_mlstm_n[:, :, :, None, :], state_mlstm_m[:, :, :, None, None])
    fw = final_norm_w[None, :]

    pool_tile = 512
    pool_p, pool_s, conv_p, conv_s = [], [], [], []
    st_p = st_s = None
    for l in range(DEPTH):
        def layer(x, pe, hist16, hist8, state, prev, *, hist_l, nseq, rows, n_valid, nc, tm_in, tn, nb,
                  prow, seq_tiles, pos0, tm_merge, act_dtype):
            pa, pb, gates = _inproj(x, norm_w3, w_in, w_tail, w_if_hi, w_if_lo, b_if_p, l=l, tm=tm_in,
                                    tn=tn, pb_dtype=act_dtype)
            a = _pool(pa, pb, hist16, pool_w_b, pool_scale3, w_pool_down_b, l=l, hist_l=hist_l, nb=nb,
                      rows=prow, seq_tiles=seq_tiles, pos0=pos0)
            hcg, c, n, m = _mlstm(pa, pb, gates, hist8, conv_w, conv_b3, mlstm_norm_w3, state, prev,
                                  l=l, hist_l=hist_l, nseq=nseq, rows=rows, n_valid=n_valid, nc=nc,
                                  hcg_dtype=act_dtype)
            x = _merge(x, a, hcg, pb, pe, w_mlstm_down_b, w_out_b, w_ple_b, ple_norm_w3, w_ple_gate_b,
                       fw, l=l, tm=tm_merge)
            return x, pa, (c, n, m)

        xp, pa_p, st_p = layer(xp, pe_p, hist16_p, hist8_p, None, st_p, hist_l=0, nseq=batch,
                               rows=CHUNK, n_valid=CHUNK, nc=seq // CHUNK, tm_in=2048, tn=512, nb=1,
                               prow=pool_tile, seq_tiles=seq // pool_tile, pos0=0, tm_merge=512,
                               act_dtype=BF16)
        pp = pa_p.reshape(batch, seq, PA_COLS)
        pool_p.append(pp[:, seq - POOL_HIST:, PA_POOL_IN:])
        conv_p.append(pp[:, seq - (CONV_W - 1):, :PA_POOL_IN])

        xs, pa_s, st_s = layer(xs, pe_s, hist16_s, hist8_s, state_s, st_s, hist_l=l, nseq=dec_batch,
                               rows=SAMPLE_ROWS, n_valid=dec_seq, nc=1, tm_in=ts, tn=512, nb=16,
                               prow=SAMPLE_ROWS, seq_tiles=1, pos0=PAST_LEN, tm_merge=512,
                               act_dtype=F32)
        sp = pa_s.reshape(dec_batch, SAMPLE_ROWS, PA_COLS)
        pool_s.append(jnp.concatenate([state_pool[l][:, dec_seq:], sp[:, :dec_seq, PA_POOL_IN:]], axis=1))
        conv_s.append(sp[:, dec_seq - (CONV_W - 1):dec_seq, :PA_POOL_IN])

    y_prompt = xp.reshape(batch, seq, D_MODEL)
    y_sample = xs.reshape(dec_batch, SAMPLE_ROWS, D_MODEL)[:, :dec_seq]
    c_p, n_p, m_p = st_p
    c_s, n_s, m_s = st_s
    return (y_prompt, y_sample, jnp.stack(pool_p), jnp.stack(pool_s), jnp.stack(conv_p),
            jnp.stack(conv_s), c_p, c_s, n_p[:, :, :, 0, :], n_s[:, :, :, 0, :],
            m_p[:, :, :, 0, 0], m_s[:, :, :, 0, 0])
```

```python
import functools

import jax
import jax.numpy as jnp
from jax import lax
from jax.experimental import pallas as pl
from jax.experimental.pallas import tpu as pltpu

F32 = jnp.float32
BF16 = jnp.bfloat16

D_MODEL = 1024
DEPTH = 4
POOL_WINDOWS = (2, 4, 8, 16)
POOL_WIDTH = D_MODEL
POOL_GROUP_DIM = POOL_WIDTH // len(POOL_WINDOWS)
POOL_HIST = max(POOL_WINDOWS) - 1
MLSTM_WIDTH = 2 * D_MODEL
N_HEADS = 4
HEAD_DIM = MLSTM_WIDTH // N_HEADS
CONV_W = 4
PLE_DIM = 256
CHUNK = 128
EPS = 1e-6
PAST_LEN = 16384

IF_OFF = 2 * POOL_WIDTH + 5 * MLSTM_WIDTH
N_GATES = 2 * N_HEADS
W_MAIN_COLS = IF_OFF
W_TAIL_COLS = 2 * D_MODEL

PB_Q, PB_K, PB_V, PB_O, PB_MZ = (i * MLSTM_WIDTH for i in range(5))
PB_GATE_A = 5 * MLSTM_WIDTH
PB_GATE_B = PB_GATE_A + D_MODEL
PB_POOL_Z = PB_GATE_B + D_MODEL
PB_COLS = PB_POOL_Z + POOL_WIDTH

LANES = 128
SUBLANES = 8
POOL_HALO = 16
CONV_HALO = SUBLANES
SAMPLE_ROWS = SUBLANES
NORM_ROWS = 256
VMEM_LIMIT = 56 * 1024 * 1024


def _sigmoid(x):
    return 0.5 * jnp.tanh(0.5 * x) + 0.5


def _silu(x):
    hx = 0.5 * x
    return hx * jnp.tanh(hx) + hx


def _log_sigmoid(x):
    return jnp.minimum(x, 0.0) - jnp.log1p(jnp.exp(-jnp.abs(x)))


def _rms(x, w):
    return x * lax.rsqrt(jnp.mean(x * x, axis=-1, keepdims=True) + EPS) * w


def _dot(a, b):
    return jnp.dot(a, b, preferred_element_type=F32)


def _params(*sem):
    return pltpu.CompilerParams(dimension_semantics=sem, vmem_limit_bytes=VMEM_LIMIT)


def _dot_t(a, b):
    return lax.dot_general(a, b, (((1,), (1,)), ((), ())), preferred_element_type=F32)


def _inproj_tiles(tn):
    u = D_MODEL // tn
    n_main = W_MAIN_COLS // tn
    n_total = n_main + W_TAIL_COLS // tn
    n_qk = 2 * MLSTM_WIDTH // tn

    def pa_idx(n):
        return jnp.minimum(n, u - 1)

    def pb_idx(n):
        return jnp.where(n < u, PB_POOL_Z // tn,
                         jnp.where(n < 2 * u, PB_POOL_Z // tn + n - u, n - 2 * u))

    def tail_idx(n):
        return jnp.clip(n - 2 * u, 0, n_qk - 1)

    return u, n_main, n_total, n_qk, pa_idx, pb_idx, tail_idx


def _inproj_kernel(*refs, tn, emit_tail):
    if emit_tail:
        (x_ref, nw_ref, w_ref, wt_ref, wih_ref, wil_ref, bif_ref, pa_ref, pb_ref, gates_ref, tail_ref,
         h_ref) = refs
    else:
        (x_ref, nw_ref, w_ref, wt_ref, wih_ref, wil_ref, bif_ref, pa_ref, pb_ref, gates_ref,
         h_ref) = refs
    u, n_main, _, n_qk, _, _, _ = _inproj_tiles(tn)
    n = pl.program_id(1)

    @pl.when(n == 0)
    def _():
        def norm_rows(r, carry):
            rs = pl.ds(pl.multiple_of(r * NORM_ROWS, NORM_ROWS), NORM_ROWS)
            h = _rms(x_ref[rs, :], nw_ref[...])
            hh = h.astype(BF16)
            hl = (h - hh.astype(F32)).astype(BF16)
            h_ref[rs, :] = hh
            gates_ref[rs, :] = (_dot_t(hh, wih_ref[...]) + _dot_t(hl, wih_ref[...])
                                + _dot_t(hh, wil_ref[...]) + bif_ref[...])
            return carry

        lax.fori_loop(0, x_ref.shape[0] // NORM_ROWS, norm_rows, 0)

    @pl.when(n < u)
    def _():
        pa_ref[...] = _dot_t(h_ref[...], w_ref[...].astype(BF16))

    in_main = (n >= u) & (n < n_main)
    if emit_tail:
        is_qk = (n >= 2 * u) & (n < 2 * u + n_qk)

        @pl.when(is_qk)
        def _():
            acc = _dot_t(h_ref[...], w_ref[...].astype(BF16))
            pb_ref[...] = acc.astype(pb_ref.dtype)
            tail_ref[...] = acc[acc.shape[0] - CONV_HALO:, :]

        in_main = in_main & jnp.logical_not(is_qk)

    @pl.when(in_main)
    def _():
        pb_ref[...] = _dot_t(h_ref[...], w_ref[...].astype(BF16)).astype(pb_ref.dtype)

    @pl.when(n >= n_main)
    def _():
        pb_ref[...] = _dot_t(h_ref[...], wt_ref[...].astype(BF16)).astype(pb_ref.dtype)


def _inproj(x, nw, w_main_t, w_tail_t, wih, wil, bif, *, l, tm, tn, pb_dtype, emit_tail):
    t = x.shape[0]
    _, n_main, n_total, _, pa_idx, pb_idx, tail_idx = _inproj_tiles(tn)
    out_specs = [
        pl.BlockSpec((tm, tn), lambda m, n: (m, pa_idx(n))),
        pl.BlockSpec((tm, tn), lambda m, n: (m, pb_idx(n))),
        pl.BlockSpec((tm, LANES), lambda m, n: (m, 0)),
    ]
    out_shape = [
        jax.ShapeDtypeStruct((t, POOL_WIDTH), F32),
        jax.ShapeDtypeStruct((t, PB_COLS), pb_dtype),
        jax.ShapeDtypeStruct((t, LANES), F32),
    ]
    if emit_tail:
        out_specs.append(pl.BlockSpec((None, CONV_HALO, tn), lambda m, n: (m, 0, tail_idx(n))))
        out_shape.append(jax.ShapeDtypeStruct((t // tm, CONV_HALO, 2 * MLSTM_WIDTH), F32))
    return pl.pallas_call(
        functools.partial(_inproj_kernel, tn=tn, emit_tail=emit_tail),
        grid=(t // tm, n_total),
        in_specs=[
            pl.BlockSpec((tm, D_MODEL), lambda m, n: (m, 0)),
            pl.BlockSpec((None, 1, D_MODEL), lambda m, n: (l, 0, 0)),
            pl.BlockSpec((None, tn, D_MODEL), lambda m, n: (l, jnp.minimum(n, n_main - 1), 0)),
            pl.BlockSpec((None, tn, D_MODEL), lambda m, n: (l, jnp.maximum(n - n_main, 0), 0)),
            pl.BlockSpec((None, LANES, D_MODEL), lambda m, n: (l, 0, 0)),
            pl.BlockSpec((None, LANES, D_MODEL), lambda m, n: (l, 0, 0)),
            pl.BlockSpec((None, 1, LANES), lambda m, n: (l, 0, 0)),
        ],
        out_specs=out_specs,
        out_shape=out_shape,
        scratch_shapes=[pltpu.VMEM((tm, D_MODEL), BF16)],
        compiler_params=_params("parallel", "arbitrary"),
        name="inproj",
    )(x, nw, w_main_t, w_tail_t, wih, wil, bif)


def _pool_kernel(*refs, nb, rows, seq_tiles, pos0):
    if seq_tiles > 1:
        u_ref, z_ref, halo_ref, hist_ref, pw_ref, ps_ref, wd_ref, a_ref, xs_ref = refs
        st = pl.program_id(0) % seq_tiles
        halo = jnp.where(st == 0, hist_ref[...], halo_ref[...].reshape(1, POOL_HALO, POOL_WIDTH))
        tile_pos0 = pos0 + st * rows
    else:
        u_ref, z_ref, hist_ref, pw_ref, ps_ref, wd_ref, a_ref, xs_ref = refs
        halo = hist_ref[...]
        tile_pos0 = pos0
    xs_ref[:, 0:POOL_HALO, :] = halo
    xs_ref[:, POOL_HALO:, :] = u_ref[...].reshape(nb, rows, POOL_WIDTH)
    ext = POOL_HALO + rows
    pos = tile_pos0 + lax.broadcasted_iota(jnp.int32, (nb, rows, POOL_GROUP_DIM), 1)
    mixed = []
    for g, win in enumerate(POOL_WINDOWS):
        cols = slice(g * POOL_GROUP_DIM, (g + 1) * POOL_GROUP_DIM)
        acc = xs_ref[:, :, cols].reshape(nb * ext, POOL_GROUP_DIM)
        span = 1
        while span < win:
            acc = acc + pltpu.roll(acc, span, axis=0)
            span *= 2
        acc = acc.reshape(nb, ext, POOL_GROUP_DIM)[:, POOL_HALO:, :]
        cnt = jnp.minimum(pos + 1, win).astype(F32)
        pooled = (acc / cnt - xs_ref[:, POOL_HALO:, cols]).reshape(nb * rows, POOL_GROUP_DIM)
        mixed.append(_dot(pooled.astype(BF16), pw_ref[g]))
    mixed = jnp.concatenate(mixed, axis=-1) * ps_ref[...]
    act = mixed * _silu(z_ref[...].astype(F32))
    a_ref[...] = _dot(act.astype(BF16), wd_ref[...])


def _pool(pa, pb, hist, pw, ps, wd, *, l, hist_l, nb, rows, seq_tiles, pos0):
    t = pa.shape[0]
    tm = nb * rows
    in_specs = [
        pl.BlockSpec((tm, POOL_WIDTH), lambda i: (i, 0)),
        pl.BlockSpec((tm, POOL_WIDTH), lambda i: (i, PB_POOL_Z // POOL_WIDTH)),
    ]
    args = [pa, pb]
    if seq_tiles > 1:
        per = rows // POOL_HALO
        in_specs.append(pl.BlockSpec((POOL_HALO, POOL_WIDTH),
                                     lambda i: (jnp.maximum(i * per - 1, 0), 0)))
        args.append(pa)
        in_specs.append(pl.BlockSpec((None, 1, POOL_HALO, POOL_WIDTH),
                                     lambda i: (hist_l, i // seq_tiles, 0, 0)))
    else:
        in_specs.append(pl.BlockSpec((None, nb, POOL_HALO, POOL_WIDTH), lambda i: (hist_l, i, 0, 0)))
    args.append(hist)
    in_specs += [
        pl.BlockSpec((None, len(POOL_WINDOWS), POOL_GROUP_DIM, POOL_GROUP_DIM), lambda i: (l, 0, 0, 0)),
        pl.BlockSpec((None, 1, POOL_WIDTH), lambda i: (l, 0, 0)),
        pl.BlockSpec((None, POOL_WIDTH, D_MODEL), lambda i: (l, 0, 0)),
    ]
    args += [pw, ps, wd]
    return pl.pallas_call(
        functools.partial(_pool_kernel, nb=nb, rows=rows, seq_tiles=seq_tiles, pos0=pos0),
        grid=(t // tm,),
        in_specs=in_specs,
        out_specs=pl.BlockSpec((tm, D_MODEL), lambda i: (i, 0)),
        out_shape=jax.ShapeDtypeStruct((t, D_MODEL), F32),
        scratch_shapes=[pltpu.VMEM((nb, POOL_HALO + rows, POOL_WIDTH), F32)],
        compiler_params=_params("parallel"),
        name="pool",
    )(*args)


def _conv_silu(halo, x, w_ref, b_ref, wcols):
    z = jnp.concatenate([halo, x], axis=0)
    acc = z * w_ref[0:1, wcols]
    for j in range(1, CONV_W):
        acc = pltpu.roll(acc, 1, axis=0) + z * w_ref[j:j + 1, wcols]
    return _silu(acc[CONV_HALO:, :] + b_ref[:, wcols])


def _mlstm_kernel(*refs, rows, n_valid, nc, zero_init, n_alias):
    refs = list(refs)
    q_ref, k_ref, v_ref, o_ref, mz_ref = refs[:5]
    refs = refs[5:]
    if nc > 1:
        qh_ref, kh_ref = refs[:2]
        refs = refs[2:]
    hist_ref, g_ref, cw_ref, cb_ref, nw_ref = refs[:5]
    refs = refs[5:]
    if not zero_init:
        c0_ref, n0_ref, m0_ref = refs[:3]
        refs = refs[3:]
    refs = refs[n_alias:]
    hcg_ref, c_ref, n_ref, m_ref = refs

    ci = pl.program_id(1)

    if zero_init:
        @pl.when(ci == 0)
        def _():
            c_ref[...] = jnp.zeros(c_ref.shape, F32)
            n_ref[...] = jnp.zeros(n_ref.shape, F32)
            m_ref[...] = jnp.zeros(m_ref.shape, F32)

    kq = slice(0, MLSTM_WIDTH)
    kk = slice(MLSTM_WIDTH, 2 * MLSTM_WIDTH)
    if nc > 1:
        first = ci == 0
        lo = qh_ref.shape[0] - CONV_HALO
        q_halo = jnp.where(first, hist_ref[0, :, kq], qh_ref[...].astype(F32)[lo:, :])
        k_halo = jnp.where(first, hist_ref[0, :, kk], kh_ref[...].astype(F32)[lo:, :])
    else:
        q_halo = hist_ref[0, :, kq]
        k_halo = hist_ref[0, :, kk]

    row = lax.broadcasted_iota(jnp.int32, (rows, rows), 0)
    col = lax.broadcasted_iota(jnp.int32, (rows, rows), 1)
    causal = col <= row
    diag = col == row
    gates = g_ref[...]

    for h in range(N_HEADS):
        hs = slice(h * HEAD_DIM, (h + 1) * HEAD_DIM)
        ks = slice(MLSTM_WIDTH + h * HEAD_DIM, MLSTM_WIDTH + (h + 1) * HEAD_DIM)
        q = _conv_silu(q_halo[:, hs], q_ref[:, hs].astype(F32), cw_ref, cb_ref, hs)
        k = _conv_silu(k_halo[:, hs], k_ref[:, hs].astype(F32), cw_ref, cb_ref, ks) * (HEAD_DIM ** -0.5)
        vb = v_ref[:, hs].astype(BF16)

        i_col = gates[:, h:h + 1]
        ls_col = _log_sigmoid(gates[:, N_HEADS + h:N_HEADS + h + 1])
        i_row = jnp.sum(jnp.where(diag, i_col, 0.0), axis=0, keepdims=True)
        ls_row = jnp.sum(jnp.where(diag, ls_col, 0.0), axis=0, keepdims=True)
        b_col = jnp.sum(jnp.where(causal, ls_row, 0.0), axis=1, keepdims=True)
        b_row = jnp.sum(jnp.where(row <= col, ls_col, 0.0), axis=0, keepdims=True)

        if zero_init:
            c, n, m = c_ref[0, h], n_ref[0, h], m_ref[0, h][:, 0:1]
        else:
            c, n, m = c0_ref[0, h], n0_ref[0, h], m0_ref[0, h]

        log_d = jnp.where(causal, b_col - b_row + i_row, -jnp.inf)
        m_inter = b_col + m
        m_t = jnp.maximum(m_inter, jnp.max(log_d, axis=-1, keepdims=True))
        d = jnp.exp(log_d - m_t)
        qb, kb = q.astype(BF16), k.astype(BF16)
        s = lax.dot_general(qb, kb, (((1,), (1,)), ((), ())), preferred_element_type=F32) * d
        g = jnp.exp(m_inter - m_t)
        num = _dot(s.astype(BF16), vb) + g * _dot(qb, c.astype(BF16))
        den = jnp.sum(s, axis=-1, keepdims=True) + g * jnp.sum(q * n, axis=-1, keepdims=True)
        hh = num / jnp.maximum(jnp.abs(den), jnp.exp(-m_t))

        b_last = b_col[n_valid - 1:n_valid, :]
        m_new = m_t[n_valid - 1:n_valid, :]
        decay = jnp.exp(b_last + m - m_new)
        w_s = jnp.exp(b_last - b_col + i_col - m_new)
        kw = k * w_s
        if n_valid < rows:
            tok = lax.broadcasted_iota(jnp.int32, (rows, HEAD_DIM), 0)
            kw = jnp.where(tok < n_valid, kw, 0.0)
        c_ref[0, h] = decay * c + lax.dot_general(kw.astype(BF16), vb, (((0,), (0,)), ((), ())),
                                                  preferred_element_type=F32)
        n_ref[0, h] = decay * n + jnp.sum(kw, axis=0, keepdims=True)
        m_ref[0, h] = jnp.broadcast_to(m_new, (1, LANES))

        hc = hh * _sigmoid(o_ref[:, hs].astype(F32))
        y = _rms(hc, nw_ref[:, hs])
        hcg_ref[:, hs] = (y * _silu(mz_ref[:, hs].astype(F32))).astype(hcg_ref.dtype)


def _mlstm(pb, gates, hist8, cw, cb, nw, state, prev, *, l, hist_l, nseq, rows, n_valid, nc,
           hcg_dtype):
    t = pb.shape[0]
    zero_init = state is None
    mw, hb = MLSTM_WIDTH, HEAD_DIM

    def tok(off):
        return pl.BlockSpec((rows, mw), lambda b, c: (b * nc + c, off // mw))

    in_specs = [tok(PB_Q), tok(PB_K), tok(PB_V), tok(PB_O), tok(PB_MZ)]
    args = [pb] * 5
    if nc > 1:
        halo_rows = CONV_HALO * (4 // pb.dtype.itemsize)
        per = rows // halo_rows

        def halo(off):
            return pl.BlockSpec((halo_rows, mw),
                                lambda b, c: (jnp.maximum((b * nc + c) * per - 1, 0), off // mw))

        in_specs += [halo(PB_Q), halo(PB_K)]
        args += [pb, pb]
    in_specs += [
        pl.BlockSpec((None, 1, CONV_HALO, 2 * mw), lambda b, c: (hist_l, b, 0, 0)),
        pl.BlockSpec((rows, LANES), lambda b, c: (b * nc + c, 0)),
        pl.BlockSpec((None, CONV_W, 2 * mw), lambda b, c: (l, 0, 0)),
        pl.BlockSpec((None, 1, 2 * mw), lambda b, c: (l, 0, 0)),
        pl.BlockSpec((None, 1, mw), lambda b, c: (l, 0, 0)),
    ]
    args += [hist8, gates, cw, cb, nw]
    c_spec = pl.BlockSpec((None, 1, N_HEADS, hb, hb), lambda b, c: (l, b, 0, 0, 0))
    n_spec = pl.BlockSpec((None, 1, N_HEADS, 1, hb), lambda b, c: (l, b, 0, 0, 0))
    if not zero_init:
        in_specs += [c_spec, n_spec,
                     pl.BlockSpec((None, 1, N_HEADS, 1, 1), lambda b, c: (l, b, 0, 0, 0))]
        args += list(state)
    aliases = {}
    if prev is not None:
        for j, p in enumerate(prev):
            aliases[len(args)] = 1 + j
            in_specs.append(pl.BlockSpec(memory_space=pl.ANY))
            args.append(p)
    n_alias = len(aliases)
    return pl.pallas_call(
        functools.partial(_mlstm_kernel, rows=rows, n_valid=n_valid, nc=nc, zero_init=zero_init,
                          n_alias=n_alias),
        grid=(nseq, nc),
        in_specs=in_specs,
        out_specs=[
            pl.BlockSpec((rows, mw), lambda b, c: (b * nc + c, 0)),
            c_spec,
            n_spec,
            pl.BlockSpec((None, 1, N_HEADS, 1, LANES), lambda b, c: (l, b, 0, 0, 0)),
        ],
        out_shape=[
            jax.ShapeDtypeStruct((t, mw), hcg_dtype),
            jax.ShapeDtypeStruct((DEPTH, nseq, N_HEADS, hb, hb), F32),
            jax.ShapeDtypeStruct((DEPTH, nseq, N_HEADS, 1, hb), F32),
            jax.ShapeDtypeStruct((DEPTH, nseq, N_HEADS, 1, LANES), F32),
        ],
        input_output_aliases=aliases,
        compiler_params=_params("parallel", "arbitrary"),
        name="mlstm",
    )(*args)


def _merge_kernel(x_ref, a_ref, hcg_ref, ga_ref, gb_ref, pe_ref, wmd_ref, wo_ref, wple_ref, pnw_ref,
                  wpg_ref, fw_ref, out_ref, *, final):
    bm = _dot(hcg_ref[...].astype(BF16), wmd_ref[...])
    mix = (_sigmoid(ga_ref[...].astype(F32)) * a_ref[...]
           + _sigmoid(gb_ref[...].astype(F32)) * bm)
    x1 = x_ref[...] + _dot(mix.astype(BF16), wo_ref[...])
    r = _rms(x1, pnw_ref[...])
    gate = _sigmoid(_dot(r.astype(BF16), wpg_ref[...]))
    x2 = x1 + _dot(pe_ref[...].astype(BF16), wple_ref[...]) * gate
    out_ref[...] = _rms(x2, fw_ref[...]) if final else x2


def _merge(x, a, hcg, pb, pe, wmd, wo, wple, pnw, wpg, fw, *, l, tm):
    t = x.shape[0]

    def lw(*shape):
        return pl.BlockSpec((None,) + shape, lambda i: (l, 0, 0))

    return pl.pallas_call(
        functools.partial(_merge_kernel, final=(l == DEPTH - 1)),
        grid=(t // tm,),
        in_specs=[
            pl.BlockSpec((tm, D_MODEL), lambda i: (i, 0)),
            pl.BlockSpec((tm, D_MODEL), lambda i: (i, 0)),
            pl.BlockSpec((tm, MLSTM_WIDTH), lambda i: (i, 0)),
            pl.BlockSpec((tm, D_MODEL), lambda i: (i, PB_GATE_A // D_MODEL)),
            pl.BlockSpec((tm, D_MODEL), lambda i: (i, PB_GATE_B // D_MODEL)),
            pl.BlockSpec((None, tm, PLE_DIM), lambda i: (l, i, 0)),
            lw(MLSTM_WIDTH, D_MODEL),
            lw(D_MODEL, D_MODEL),
            lw(PLE_DIM, D_MODEL),
            lw(1, D_MODEL),
            lw(D_MODEL, D_MODEL),
            pl.BlockSpec((1, D_MODEL), lambda i: (0, 0)),
        ],
        out_specs=pl.BlockSpec((tm, D_MODEL), lambda i: (i, 0)),
        out_shape=jax.ShapeDtypeStruct((t, D_MODEL), F32),
        compiler_params=_params("parallel"),
        name="merge",
    )(x, a, hcg, pb, pb, pe, wmd, wo, wple, pnw, wpg, fw)


def kernel(x_prompt, x_sample, state_pool, state_conv, state_mlstm_C, state_mlstm_n, state_mlstm_m,
           p_prompt, p_sample, norm_w, w_in, b_if, conv_w, conv_b, pool_w, pool_scale, w_pool_down,
           mlstm_norm_w, w_mlstm_down, w_out, w_ple, ple_norm_w, w_ple_gate, final_norm_w):
    batch, seq, _ = x_prompt.shape
    dec_batch, dec_seq, _ = x_sample.shape
    assert seq % CHUNK == 0 and CONV_W - 1 <= dec_seq <= SAMPLE_ROWS
    pad_rows = SAMPLE_ROWS - dec_seq
    tp = batch * seq
    ts = dec_batch * SAMPLE_ROWS

    w_in_t = jnp.swapaxes(w_in, 1, 2)
    w_tail_t = w_in_t[:, IF_OFF + N_GATES:, :]
    w_if = jnp.pad(w_in_t[:, IF_OFF:IF_OFF + N_GATES, :], ((0, 0), (0, LANES - N_GATES), (0, 0)))
    w_if_hi = w_if.astype(BF16)
    w_if_lo = (w_if - w_if_hi.astype(F32)).astype(BF16)
    b_if_p = jnp.pad(b_if, ((0, 0), (0, LANES - N_GATES)))[:, None, :]
    pool_w_b = pool_w.astype(BF16)
    w_pool_down_b = w_pool_down.astype(BF16)
    w_mlstm_down_b = w_mlstm_down.astype(BF16)
    w_out_b = w_out.astype(BF16)
    w_ple_b = w_ple.astype(BF16)
    w_ple_gate_b = w_ple_gate.astype(BF16)
    norm_w3, conv_b3, pool_scale3 = norm_w[:, None, :], conv_b[:, None, :], pool_scale[:, None, :]
    mlstm_norm_w3, ple_norm_w3 = mlstm_norm_w[:, None, :], ple_norm_w[:, None, :]

    xp = x_prompt.reshape(tp, D_MODEL)
    xs = jnp.pad(x_sample, ((0, 0), (0, pad_rows), (0, 0))).reshape(ts, D_MODEL)
    pe_s = jnp.pad(p_sample, ((0, 0), (0, 0), (0, pad_rows), (0, 0))).reshape(DEPTH, ts, PLE_DIM)
    pe_p = p_prompt.reshape(DEPTH, tp, PLE_DIM)
    hist16_s = jnp.pad(state_pool, ((0, 0), (0, 0), (POOL_HALO - POOL_HIST, 0), (0, 0)))
    hist8_s = jnp.pad(state_conv, ((0, 0), (0, 0), (CONV_HALO - (CONV_W - 1), 0), (0, 0)))
    hist16_p = jnp.zeros((1, batch, POOL_HALO, POOL_WIDTH), F32)
    hist8_p = jnp.zeros((1, batch, CONV_HALO, 2 * MLSTM_WIDTH), F32)
    state_s = (state_mlstm_C, state_mlstm_n[:, :, :, None, :], state_mlstm_m[:, :, :, None, None])
    fw = final_norm_w[None, :]

    pool_tile = 512
    pool_p, pool_s, conv_p, conv_s = [], [], [], []
    st_p = st_s = None
    for l in range(DEPTH):
        def layer(x, pe, hist16, hist8, state, prev, *, hist_l, nseq, rows, n_valid, nc, tm_in, tn, nb,
                  prow, seq_tiles, pos0, tm_merge, act_dtype, emit_tail):
            proj = _inproj(x, norm_w3, w_in_t, w_tail_t, w_if_hi, w_if_lo, b_if_p, l=l, tm=tm_in, tn=tn,
                           pb_dtype=act_dtype, emit_tail=emit_tail)
            pa, pb, gates = proj[:3]
            a = _pool(pa, pb, hist16, pool_w_b, pool_scale3, w_pool_down_b, l=l, hist_l=hist_l, nb=nb,
                      rows=prow, seq_tiles=seq_tiles, pos0=pos0)
            hcg, c, n, m = _mlstm(pb, gates, hist8, conv_w, conv_b3, mlstm_norm_w3, state, prev,
                                  l=l, hist_l=hist_l, nseq=nseq, rows=rows, n_valid=n_valid, nc=nc,
                                  hcg_dtype=act_dtype)
            x = _merge(x, a, hcg, pb, pe, w_mlstm_down_b, w_out_b, w_ple_b, ple_norm_w3, w_ple_gate_b,
                       fw, l=l, tm=tm_merge)
            return x, proj, (c, n, m)

        xp, proj_p, st_p = layer(xp, pe_p, hist16_p, hist8_p, None, st_p, hist_l=0, nseq=batch,
                                 rows=CHUNK, n_valid=CHUNK, nc=seq // CHUNK, tm_in=seq, tn=512, nb=1,
                                 prow=pool_tile, seq_tiles=seq // pool_tile, pos0=0, tm_merge=512,
                                 act_dtype=BF16, emit_tail=True)
        pool_p.append(proj_p[0].reshape(batch, seq, POOL_WIDTH)[:, seq - POOL_HIST:])
        conv_p.append(proj_p[3][:, CONV_HALO - (CONV_W - 1):])

        xs, proj_s, st_s = layer(xs, pe_s, hist16_s, hist8_s, state_s, st_s, hist_l=l, nseq=dec_batch,
                                 rows=SAMPLE_ROWS, n_valid=dec_seq, nc=1, tm_in=ts, tn=512, nb=16,
                                 prow=SAMPLE_ROWS, seq_tiles=1, pos0=PAST_LEN, tm_merge=512,
                                 act_dtype=F32, emit_tail=False)
        sa = proj_s[0].reshape(dec_batch, SAMPLE_ROWS, POOL_WIDTH)
        sb = proj_s[1].reshape(dec_batch, SAMPLE_ROWS, PB_COLS)
        pool_s.append(jnp.concatenate([state_pool[l][:, dec_seq:], sa[:, :dec_seq]], axis=1))
        conv_s.append(sb[:, dec_seq - (CONV_W - 1):dec_seq, PB_Q:PB_Q + 2 * MLSTM_WIDTH])

    y_prompt = xp.reshape(batch, seq, D_MODEL)
    y_sample = xs.reshape(dec_batch, SAMPLE_ROWS, D_MODEL)[:, :dec_seq]
    c_p, n_p, m_p = st_p
    c_s, n_s, m_s = st_s
    return (y_prompt, y_sample, jnp.stack(pool_p), jnp.stack(pool_s), jnp.stack(conv_p),
            jnp.stack(conv_s), c_p, c_s, n_p[:, :, :, 0, :], n_s[:, :, :, 0, :],
            m_p[:, :, :, 0, 0], m_s[:, :, :, 0, 0])
```

```python
import functools

import jax
import jax.numpy as jnp
from jax import lax
from jax.experimental import pallas as pl
from jax.experimental.pallas import tpu as pltpu

F32 = jnp.float32
BF16 = jnp.bfloat16

D_MODEL = 1024
DEPTH = 4
POOL_WINDOWS = (2, 4, 8, 16)
POOL_WIDTH = D_MODEL
POOL_GROUP_DIM = POOL_WIDTH // len(POOL_WINDOWS)
POOL_HIST = max(POOL_WINDOWS) - 1
MLSTM_WIDTH = 2 * D_MODEL
N_HEADS = 4
HEAD_DIM = MLSTM_WIDTH // N_HEADS
CONV_W = 4
PLE_DIM = 256
CHUNK = 128
EPS = 1e-6
PAST_LEN = 16384

IF_OFF = 2 * POOL_WIDTH + 5 * MLSTM_WIDTH
N_GATES = 2 * N_HEADS
W_MAIN_COLS = IF_OFF
W_TAIL_COLS = 2 * D_MODEL

PB_Q, PB_K, PB_V, PB_O, PB_MZ = (i * MLSTM_WIDTH for i in range(5))
PB_GATE_A = 5 * MLSTM_WIDTH
PB_GATE_B = PB_GATE_A + D_MODEL
PB_POOL_Z = PB_GATE_B + D_MODEL
PB_COLS = PB_POOL_Z + POOL_WIDTH

LANES = 128
SUBLANES = 8
POOL_HALO = 16
CONV_HALO = SUBLANES
SAMPLE_ROWS = SUBLANES
NORM_ROWS = 256
VMEM_LIMIT = 56 * 1024 * 1024
N_STATE = 3


def _sigmoid(x):
    return 0.5 * jnp.tanh(0.5 * x) + 0.5


def _silu(x):
    hx = 0.5 * x
    return hx * jnp.tanh(hx) + hx


def _log_sigmoid(x):
    return jnp.minimum(x, 0.0) - jnp.log1p(jnp.exp(-jnp.abs(x)))


def _rms(x, w):
    return x * lax.rsqrt(jnp.mean(x * x, axis=-1, keepdims=True) + EPS) * w


def _dot(a, b):
    return jnp.dot(a, b, preferred_element_type=F32)


def _params(*sem):
    return pltpu.CompilerParams(dimension_semantics=sem, vmem_limit_bytes=VMEM_LIMIT)


def _dot_t(a, b):
    return lax.dot_general(a, b, (((1,), (1,)), ((), ())), preferred_element_type=F32)


def _inproj_tiles(tn):
    u = D_MODEL // tn
    n_main = W_MAIN_COLS // tn
    n_total = n_main + W_TAIL_COLS // tn
    n_qk = 2 * MLSTM_WIDTH // tn

    def pa_idx(n):
        return jnp.minimum(n, u - 1)

    def pb_idx(n):
        return jnp.where(n < u, PB_POOL_Z // tn,
                         jnp.where(n < 2 * u, PB_POOL_Z // tn + n - u, n - 2 * u))

    def tail_idx(n):
        return jnp.clip(n - 2 * u, 0, n_qk - 1)

    return u, n_main, n_total, n_qk, pa_idx, pb_idx, tail_idx


def _inproj_kernel(*refs, tn, emit_tail):
    if emit_tail:
        (x_ref, nw_ref, w_ref, wt_ref, wih_ref, wil_ref, bif_ref, pa_ref, pb_ref, gates_ref, tail_ref,
         h_ref) = refs
    else:
        (x_ref, nw_ref, w_ref, wt_ref, wih_ref, wil_ref, bif_ref, pa_ref, pb_ref, gates_ref,
         h_ref) = refs
    u, n_main, _, n_qk, _, _, _ = _inproj_tiles(tn)
    n = pl.program_id(1)

    @pl.when(n == 0)
    def _():
        def norm_rows(r, carry):
            rs = pl.ds(pl.multiple_of(r * NORM_ROWS, NORM_ROWS), NORM_ROWS)
            h = _rms(x_ref[rs, :], nw_ref[...])
            hh = h.astype(BF16)
            hl = (h - hh.astype(F32)).astype(BF16)
            h_ref[rs, :] = hh
            gates_ref[rs, :] = (_dot_t(hh, wih_ref[...]) + _dot_t(hl, wih_ref[...])
                                + _dot_t(hh, wil_ref[...]) + bif_ref[...])
            return carry

        lax.fori_loop(0, x_ref.shape[0] // NORM_ROWS, norm_rows, 0)

    @pl.when(n < u)
    def _():
        pa_ref[...] = _dot_t(h_ref[...], w_ref[...].astype(BF16))

    in_main = (n >= u) & (n < n_main)
    if emit_tail:
        is_qk = (n >= 2 * u) & (n < 2 * u + n_qk)

        @pl.when(is_qk)
        def _():
            acc = _dot_t(h_ref[...], w_ref[...].astype(BF16))
            pb_ref[...] = acc.astype(pb_ref.dtype)
            tail_ref[...] = acc[acc.shape[0] - CONV_HALO:, :]

        in_main = in_main & jnp.logical_not(is_qk)

    @pl.when(in_main)
    def _():
        pb_ref[...] = _dot_t(h_ref[...], w_ref[...].astype(BF16)).astype(pb_ref.dtype)

    @pl.when(n >= n_main)
    def _():
        pb_ref[...] = _dot_t(h_ref[...], wt_ref[...].astype(BF16)).astype(pb_ref.dtype)


def _inproj(x, nw, w_main_t, w_tail_t, wih, wil, bif, *, l, tm, tn, pb_dtype, emit_tail):
    t = x.shape[0]
    _, n_main, n_total, _, pa_idx, pb_idx, tail_idx = _inproj_tiles(tn)
    out_specs = [
        pl.BlockSpec((tm, tn), lambda m, n: (m, pa_idx(n))),
        pl.BlockSpec((tm, tn), lambda m, n: (m, pb_idx(n))),
        pl.BlockSpec((tm, LANES), lambda m, n: (m, 0)),
    ]
    out_shape = [
        jax.ShapeDtypeStruct((t, POOL_WIDTH), F32),
        jax.ShapeDtypeStruct((t, PB_COLS), pb_dtype),
        jax.ShapeDtypeStruct((t, LANES), F32),
    ]
    if emit_tail:
        out_specs.append(pl.BlockSpec((None, CONV_HALO, tn), lambda m, n: (m, 0, tail_idx(n))))
        out_shape.append(jax.ShapeDtypeStruct((t // tm, CONV_HALO, 2 * MLSTM_WIDTH), F32))
    return pl.pallas_call(
        functools.partial(_inproj_kernel, tn=tn, emit_tail=emit_tail),
        grid=(t // tm, n_total),
        in_specs=[
            pl.BlockSpec((tm, D_MODEL), lambda m, n: (m, 0)),
            pl.BlockSpec((None, 1, D_MODEL), lambda m, n: (l, 0, 0)),
            pl.BlockSpec((None, tn, D_MODEL), lambda m, n: (l, jnp.minimum(n, n_main - 1), 0)),
            pl.BlockSpec((None, tn, D_MODEL), lambda m, n: (l, jnp.maximum(n - n_main, 0), 0)),
            pl.BlockSpec((None, LANES, D_MODEL), lambda m, n: (l, 0, 0)),
            pl.BlockSpec((None, LANES, D_MODEL), lambda m, n: (l, 0, 0)),
            pl.BlockSpec((None, 1, LANES), lambda m, n: (l, 0, 0)),
        ],
        out_specs=out_specs,
        out_shape=out_shape,
        scratch_shapes=[pltpu.VMEM((tm, D_MODEL), BF16)],
        compiler_params=_params("parallel", "arbitrary"),
        name="inproj",
    )(x, nw, w_main_t, w_tail_t, wih, wil, bif)


def _pool_kernel(*refs, nb, rows, seq_tiles, pos0):
    if seq_tiles > 1:
        u_ref, z_ref, halo_ref, hist_ref, pw_ref, ps_ref, wd_ref, a_ref, xs_ref = refs
        st = pl.program_id(0) % seq_tiles
        halo = jnp.where(st == 0, hist_ref[...], halo_ref[...].reshape(1, POOL_HALO, POOL_WIDTH))
        tile_pos0 = pos0 + st * rows
    else:
        u_ref, z_ref, hist_ref, pw_ref, ps_ref, wd_ref, a_ref, xs_ref = refs
        halo = hist_ref[...]
        tile_pos0 = pos0
    xs_ref[:, 0:POOL_HALO, :] = halo
    xs_ref[:, POOL_HALO:, :] = u_ref[...].reshape(nb, rows, POOL_WIDTH)
    ext = POOL_HALO + rows
    pos = tile_pos0 + lax.broadcasted_iota(jnp.int32, (nb, rows, POOL_GROUP_DIM), 1)
    mixed = []
    for g, win in enumerate(POOL_WINDOWS):
        cols = slice(g * POOL_GROUP_DIM, (g + 1) * POOL_GROUP_DIM)
        acc = xs_ref[:, :, cols].reshape(nb * ext, POOL_GROUP_DIM)
        span = 1
        while span < win:
            acc = acc + pltpu.roll(acc, span, axis=0)
            span *= 2
        acc = acc.reshape(nb, ext, POOL_GROUP_DIM)[:, POOL_HALO:, :]
        cnt = jnp.minimum(pos + 1, win).astype(F32)
        pooled = (acc / cnt - xs_ref[:, POOL_HALO:, cols]).reshape(nb * rows, POOL_GROUP_DIM)
        mixed.append(_dot(pooled.astype(BF16), pw_ref[g]))
    mixed = jnp.concatenate(mixed, axis=-1) * ps_ref[...]
    act = mixed * _silu(z_ref[...].astype(F32))
    a_ref[...] = _dot(act.astype(BF16), wd_ref[...])


def _pool(pa, pb, hist, pw, ps, wd, *, l, hist_l, nb, rows, seq_tiles, pos0):
    t = pa.shape[0]
    tm = nb * rows
    in_specs = [
        pl.BlockSpec((tm, POOL_WIDTH), lambda i: (i, 0)),
        pl.BlockSpec((tm, POOL_WIDTH), lambda i: (i, PB_POOL_Z // POOL_WIDTH)),
    ]
    args = [pa, pb]
    if seq_tiles > 1:
        per = rows // POOL_HALO
        in_specs.append(pl.BlockSpec((POOL_HALO, POOL_WIDTH),
                                     lambda i: (jnp.maximum(i * per - 1, 0), 0)))
        args.append(pa)
        in_specs.append(pl.BlockSpec((None, 1, POOL_HALO, POOL_WIDTH),
                                     lambda i: (hist_l, i // seq_tiles, 0, 0)))
    else:
        in_specs.append(pl.BlockSpec((None, nb, POOL_HALO, POOL_WIDTH), lambda i: (hist_l, i, 0, 0)))
    args.append(hist)
    in_specs += [
        pl.BlockSpec((None, len(POOL_WINDOWS), POOL_GROUP_DIM, POOL_GROUP_DIM), lambda i: (l, 0, 0, 0)),
        pl.BlockSpec((None, 1, POOL_WIDTH), lambda i: (l, 0, 0)),
        pl.BlockSpec((None, POOL_WIDTH, D_MODEL), lambda i: (l, 0, 0)),
    ]
    args += [pw, ps, wd]
    return pl.pallas_call(
        functools.partial(_pool_kernel, nb=nb, rows=rows, seq_tiles=seq_tiles, pos0=pos0),
        grid=(t // tm,),
        in_specs=in_specs,
        out_specs=pl.BlockSpec((tm, D_MODEL), lambda i: (i, 0)),
        out_shape=jax.ShapeDtypeStruct((t, D_MODEL), F32),
        scratch_shapes=[pltpu.VMEM((nb, POOL_HALO + rows, POOL_WIDTH), F32)],
        compiler_params=_params("parallel"),
        name="pool",
    )(*args)


def _conv_silu(halo, x, w_ref, b_ref, wcols):
    z = jnp.concatenate([halo, x], axis=0)
    acc = z * w_ref[0:1, wcols]
    for j in range(1, CONV_W):
        acc = pltpu.roll(acc, 1, axis=0) + z * w_ref[j:j + 1, wcols]
    return _silu(acc[CONV_HALO:, :] + b_ref[:, wcols])


def _mlstm_stream(ins, outs, first, *, rows, n_valid, nc, zero_init):
    ins = list(ins)
    q_ref, k_ref, v_ref, o_ref, mz_ref = ins[:5]
    ins = ins[5:]
    if nc > 1:
        qh_ref, kh_ref = ins[:2]
        ins = ins[2:]
    hist_ref, g_ref, cw_ref, cb_ref, nw_ref = ins[:5]
    ins = ins[5:]
    if not zero_init:
        c0_ref, n0_ref, m0_ref = ins[:3]
    hcg_ref, c_ref, n_ref, m_ref = outs

    if zero_init:
        @pl.when(first)
        def _():
            c_ref[...] = jnp.zeros(c_ref.shape, F32)
            n_ref[...] = jnp.zeros(n_ref.shape, F32)
            m_ref[...] = jnp.zeros(m_ref.shape, F32)

    kq = slice(0, MLSTM_WIDTH)
    kk = slice(MLSTM_WIDTH, 2 * MLSTM_WIDTH)
    if nc > 1:
        lo = qh_ref.shape[0] - CONV_HALO
        q_halo = jnp.where(first, hist_ref[0, :, kq], qh_ref[...].astype(F32)[lo:, :])
        k_halo = jnp.where(first, hist_ref[0, :, kk], kh_ref[...].astype(F32)[lo:, :])
    else:
        q_halo = hist_ref[0, :, kq]
        k_halo = hist_ref[0, :, kk]

    row = lax.broadcasted_iota(jnp.int32, (rows, rows), 0)
    col = lax.broadcasted_iota(jnp.int32, (rows, rows), 1)
    causal = col <= row
    diag = col == row
    gates = g_ref[...]

    for h in range(N_HEADS):
        hs = slice(h * HEAD_DIM, (h + 1) * HEAD_DIM)
        ks = slice(MLSTM_WIDTH + h * HEAD_DIM, MLSTM_WIDTH + (h + 1) * HEAD_DIM)
        q = _conv_silu(q_halo[:, hs], q_ref[:, hs].astype(F32), cw_ref, cb_ref, hs)
        k = _conv_silu(k_halo[:, hs], k_ref[:, hs].astype(F32), cw_ref, cb_ref, ks) * (HEAD_DIM ** -0.5)
        vb = v_ref[:, hs].astype(BF16)

        i_col = gates[:, h:h + 1]
        ls_col = _log_sigmoid(gates[:, N_HEADS + h:N_HEADS + h + 1])
        i_row = jnp.sum(jnp.where(diag, i_col, 0.0), axis=0, keepdims=True)
        ls_row = jnp.sum(jnp.where(diag, ls_col, 0.0), axis=0, keepdims=True)
        b_col = jnp.sum(jnp.where(causal, ls_row, 0.0), axis=1, keepdims=True)
        b_row = jnp.sum(jnp.where(row <= col, ls_col, 0.0), axis=0, keepdims=True)

        if zero_init:
            c, n, m = c_ref[0, h], n_ref[0, h], m_ref[0, h][:, 0:1]
        else:
            c, n, m = c0_ref[0, h], n0_ref[0, h], m0_ref[0, h]

        log_d = jnp.where(causal, b_col - b_row + i_row, -jnp.inf)
        m_inter = b_col + m
        m_t = jnp.maximum(m_inter, jnp.max(log_d, axis=-1, keepdims=True))
        d = jnp.exp(log_d - m_t)
        qb, kb = q.astype(BF16), k.astype(BF16)
        s = lax.dot_general(qb, kb, (((1,), (1,)), ((), ())), preferred_element_type=F32) * d
        g = jnp.exp(m_inter - m_t)
        num = _dot(s.astype(BF16), vb) + g * _dot(qb, c.astype(BF16))
        den = jnp.sum(s, axis=-1, keepdims=True) + g * jnp.sum(q * n, axis=-1, keepdims=True)
        hh = num / jnp.maximum(jnp.abs(den), jnp.exp(-m_t))

        b_last = b_col[n_valid - 1:n_valid, :]
        m_new = m_t[n_valid - 1:n_valid, :]
        decay = jnp.exp(b_last + m - m_new)
        w_s = jnp.exp(b_last - b_col + i_col - m_new)
        kw = k * w_s
        if n_valid < rows:
            tok = lax.broadcasted_iota(jnp.int32, (rows, HEAD_DIM), 0)
            kw = jnp.where(tok < n_valid, kw, 0.0)
        c_ref[0, h] = decay * c + lax.dot_general(kw.astype(BF16), vb, (((0,), (0,)), ((), ())),
                                                  preferred_element_type=F32)
        n_ref[0, h] = decay * n + jnp.sum(kw, axis=0, keepdims=True)
        m_ref[0, h] = jnp.broadcast_to(m_new, (1, LANES))

        hc = hh * _sigmoid(o_ref[:, hs].astype(F32))
        y = _rms(hc, nw_ref[:, hs])
        hcg_ref[:, hs] = (y * _silu(mz_ref[:, hs].astype(F32))).astype(hcg_ref.dtype)


def _mlstm_specs(pb, gates, hist8, cw, cb, nw, state, *, l, hist_l, nseq, rows, nc, hcg_dtype, where):
    t = pb.shape[0]
    mw, hb = MLSTM_WIDTH, HEAD_DIM

    def blk(b, c):
        s, ch = where(b, c)
        return s * nc + ch

    def seq(b, c):
        return where(b, c)[0]

    def tok(off):
        return pl.BlockSpec((rows, mw), lambda b, c: (blk(b, c), off // mw))

    in_specs = [tok(PB_Q), tok(PB_K), tok(PB_V), tok(PB_O), tok(PB_MZ)]
    args = [pb] * 5
    if nc > 1:
        halo_rows = CONV_HALO * (4 // pb.dtype.itemsize)
        per = rows // halo_rows

        def halo(off):
            return pl.BlockSpec((halo_rows, mw),
                                lambda b, c: (jnp.maximum(blk(b, c) * per - 1, 0), off // mw))

        in_specs += [halo(PB_Q), halo(PB_K)]
        args += [pb, pb]
    in_specs += [
        pl.BlockSpec((None, 1, CONV_HALO, 2 * mw), lambda b, c: (hist_l, seq(b, c), 0, 0)),
        pl.BlockSpec((rows, LANES), lambda b, c: (blk(b, c), 0)),
        pl.BlockSpec((None, CONV_W, 2 * mw), lambda b, c: (l, 0, 0)),
        pl.BlockSpec((None, 1, 2 * mw), lambda b, c: (l, 0, 0)),
        pl.BlockSpec((None, 1, mw), lambda b, c: (l, 0, 0)),
    ]
    args += [hist8, gates, cw, cb, nw]
    c_spec = pl.BlockSpec((None, 1, N_HEADS, hb, hb), lambda b, c: (l, seq(b, c), 0, 0, 0))
    n_spec = pl.BlockSpec((None, 1, N_HEADS, 1, hb), lambda b, c: (l, seq(b, c), 0, 0, 0))
    if state is not None:
        in_specs += [c_spec, n_spec,
                     pl.BlockSpec((None, 1, N_HEADS, 1, 1), lambda b, c: (l, seq(b, c), 0, 0, 0))]
        args += list(state)
    out_specs = [
        pl.BlockSpec((rows, mw), lambda b, c: (blk(b, c), 0)),
        c_spec,
        n_spec,
        pl.BlockSpec((None, 1, N_HEADS, 1, LANES), lambda b, c: (l, seq(b, c), 0, 0, 0)),
    ]
    out_shape = [
        jax.ShapeDtypeStruct((t, mw), hcg_dtype),
        jax.ShapeDtypeStruct((DEPTH, nseq, N_HEADS, hb, hb), F32),
        jax.ShapeDtypeStruct((DEPTH, nseq, N_HEADS, 1, hb), F32),
        jax.ShapeDtypeStruct((DEPTH, nseq, N_HEADS, 1, LANES), F32),
    ]
    return in_specs, args, out_specs, out_shape


def _mlstm_pair_kernel(*refs, n_in, n_alias, kw_p, kw_s):
    n_out = 1 + N_STATE
    ins_p = refs[:n_in[0]]
    ins_s = refs[n_in[0]:n_in[0] + n_in[1]]
    outs = refs[n_in[0] + n_in[1] + n_alias:]
    first = pl.program_id(1) == 0
    _mlstm_stream(ins_p, outs[:n_out], first, **kw_p)
    _mlstm_stream(ins_s, outs[n_out:2 * n_out], True, **kw_s)


def _mlstm_pair(prompt, sample, prev_p, prev_s, *, nseq_p, nc_p):
    (in_p, args_p, out_p, shape_p), kw_p = prompt
    (in_s, args_s, out_s, shape_s), kw_s = sample
    in_specs, args = in_p + in_s, args_p + args_s
    n_out = 1 + N_STATE
    aliases = {}
    for base, prev in ((0, prev_p), (n_out, prev_s)):
        if prev is not None:
            for j, p in enumerate(prev):
                aliases[len(args)] = base + 1 + j
                in_specs.append(pl.BlockSpec(memory_space=pl.ANY))
                args.append(p)
    outs = pl.pallas_call(
        functools.partial(_mlstm_pair_kernel, n_in=(len(in_p), len(in_s)), n_alias=len(aliases),
                          kw_p=kw_p, kw_s=kw_s),
        grid=(nseq_p, nc_p),
        in_specs=in_specs,
        out_specs=out_p + out_s,
        out_shape=shape_p + shape_s,
        input_output_aliases=aliases,
        compiler_params=_params("parallel", "arbitrary"),
        name="mlstm",
    )(*args)
    return outs[:n_out], outs[n_out:]


def _merge_kernel(x_ref, a_ref, hcg_ref, ga_ref, gb_ref, pe_ref, wmd_ref, wo_ref, wple_ref, pnw_ref,
                  wpg_ref, fw_ref, out_ref, *, final):
    bm = _dot(hcg_ref[...].astype(BF16), wmd_ref[...])
    mix = (_sigmoid(ga_ref[...].astype(F32)) * a_ref[...]
           + _sigmoid(gb_ref[...].astype(F32)) * bm)
    x1 = x_ref[...] + _dot(mix.astype(BF16), wo_ref[...])
    r = _rms(x1, pnw_ref[...])
    gate = _sigmoid(_dot(r.astype(BF16), wpg_ref[...]))
    x2 = x1 + _dot(pe_ref[...].astype(BF16), wple_ref[...]) * gate
    out_ref[...] = _rms(x2, fw_ref[...]) if final else x2


def _merge(x, a, hcg, pb, pe, wmd, wo, wple, pnw, wpg, fw, *, l, tm):
    t = x.shape[0]

    def lw(*shape):
        return pl.BlockSpec((None,) + shape, lambda i: (l, 0, 0))

    return pl.pallas_call(
        functools.partial(_merge_kernel, final=(l == DEPTH - 1)),
        grid=(t // tm,),
        in_specs=[
            pl.BlockSpec((tm, D_MODEL), lambda i: (i, 0)),
            pl.BlockSpec((tm, D_MODEL), lambda i: (i, 0)),
            pl.BlockSpec((tm, MLSTM_WIDTH), lambda i: (i, 0)),
            pl.BlockSpec((tm, D_MODEL), lambda i: (i, PB_GATE_A // D_MODEL)),
            pl.BlockSpec((tm, D_MODEL), lambda i: (i, PB_GATE_B // D_MODEL)),
            pl.BlockSpec((None, tm, PLE_DIM), lambda i: (l, i, 0)),
            lw(MLSTM_WIDTH, D_MODEL),
            lw(D_MODEL, D_MODEL),
            lw(PLE_DIM, D_MODEL),
            lw(1, D_MODEL),
            lw(D_MODEL, D_MODEL),
            pl.BlockSpec((1, D_MODEL), lambda i: (0, 0)),
        ],
        out_specs=pl.BlockSpec((tm, D_MODEL), lambda i: (i, 0)),
        out_shape=jax.ShapeDtypeStruct((t, D_MODEL), F32),
        compiler_params=_params("parallel"),
        name="merge",
    )(x, a, hcg, pb, pb, pe, wmd, wo, wple, pnw, wpg, fw)


def kernel(x_prompt, x_sample, state_pool, state_conv, state_mlstm_C, state_mlstm_n, state_mlstm_m,
           p_prompt, p_sample, norm_w, w_in, b_if, conv_w, conv_b, pool_w, pool_scale, w_pool_down,
           mlstm_norm_w, w_mlstm_down, w_out, w_ple, ple_norm_w, w_ple_gate, final_norm_w):
    batch, seq, _ = x_prompt.shape
    dec_batch, dec_seq, _ = x_sample.shape
    nc = seq // CHUNK
    assert seq % CHUNK == 0 and CONV_W - 1 <= dec_seq <= SAMPLE_ROWS
    assert dec_batch == batch * nc
    pad_rows = SAMPLE_ROWS - dec_seq
    tp = batch * seq
    ts = dec_batch * SAMPLE_ROWS

    w_in_t = jnp.swapaxes(w_in, 1, 2)
    w_tail_t = w_in_t[:, IF_OFF + N_GATES:, :]
    w_if = jnp.pad(w_in_t[:, IF_OFF:IF_OFF + N_GATES, :], ((0, 0), (0, LANES - N_GATES), (0, 0)))
    w_if_hi = w_if.astype(BF16)
    w_if_lo = (w_if - w_if_hi.astype(F32)).astype(BF16)
    b_if_p = jnp.pad(b_if, ((0, 0), (0, LANES - N_GATES)))[:, None, :]
    pool_w_b = pool_w.astype(BF16)
    w_pool_down_b = w_pool_down.astype(BF16)
    w_mlstm_down_b = w_mlstm_down.astype(BF16)
    w_out_b = w_out.astype(BF16)
    w_ple_b = w_ple.astype(BF16)
    w_ple_gate_b = w_ple_gate.astype(BF16)
    norm_w3, conv_b3, pool_scale3 = norm_w[:, None, :], conv_b[:, None, :], pool_scale[:, None, :]
    mlstm_norm_w3, ple_norm_w3 = mlstm_norm_w[:, None, :], ple_norm_w[:, None, :]

    xp = x_prompt.reshape(tp, D_MODEL)
    xs = jnp.pad(x_sample, ((0, 0), (0, pad_rows), (0, 0))).reshape(ts, D_MODEL)
    pe_s = jnp.pad(p_sample, ((0, 0), (0, 0), (0, pad_rows), (0, 0))).reshape(DEPTH, ts, PLE_DIM)
    pe_p = p_prompt.reshape(DEPTH, tp, PLE_DIM)
    hist16_s = jnp.pad(state_pool, ((0, 0), (0, 0), (POOL_HALO - POOL_HIST, 0), (0, 0)))
    hist8_s = jnp.pad(state_conv, ((0, 0), (0, 0), (CONV_HALO - (CONV_W - 1), 0), (0, 0)))
    hist16_p = jnp.zeros((1, batch, POOL_HALO, POOL_WIDTH), F32)
    hist8_p = jnp.zeros((1, batch, CONV_HALO, 2 * MLSTM_WIDTH), F32)
    state_s = (state_mlstm_C, state_mlstm_n[:, :, :, None, :], state_mlstm_m[:, :, :, None, None])
    fw = final_norm_w[None, :]

    pool_tile = 512
    tn = 512
    tm_merge = 512
    pool_p, pool_s, conv_p, conv_s = [], [], [], []
    st_p = st_s = None
    for l in range(DEPTH):
        def project(x, *, tm, dtype, emit_tail):
            return _inproj(x, norm_w3, w_in_t, w_tail_t, w_if_hi, w_if_lo, b_if_p, l=l, tm=tm, tn=tn,
                           pb_dtype=dtype, emit_tail=emit_tail)

        def pool(pa, pb, hist16, **kw):
            return _pool(pa, pb, hist16, pool_w_b, pool_scale3, w_pool_down_b, l=l, **kw)

        def merge(x, a, hcg, pb, pe):
            return _merge(x, a, hcg, pb, pe, w_mlstm_down_b, w_out_b, w_ple_b, ple_norm_w3, w_ple_gate_b,
                          fw, l=l, tm=tm_merge)

        pa_p, pb_p, gates_p, tail_p = project(xp, tm=seq, dtype=BF16, emit_tail=True)
        pa_s, pb_s, gates_s = project(xs, tm=ts, dtype=F32, emit_tail=False)
        a_p = pool(pa_p, pb_p, hist16_p, hist_l=0, nb=1, rows=pool_tile, seq_tiles=seq // pool_tile,
                   pos0=0)
        a_s = pool(pa_s, pb_s, hist16_s, hist_l=l, nb=16, rows=SAMPLE_ROWS, seq_tiles=1, pos0=PAST_LEN)

        specs_p = _mlstm_specs(pb_p, gates_p, hist8_p, conv_w, conv_b3, mlstm_norm_w3, None, l=l,
                               hist_l=0, nseq=batch, rows=CHUNK, nc=nc, hcg_dtype=BF16,
                               where=lambda b, c: (b, c))
        specs_s = _mlstm_specs(pb_s, gates_s, hist8_s, conv_w, conv_b3, mlstm_norm_w3, state_s, l=l,
                               hist_l=l, nseq=dec_batch, rows=SAMPLE_ROWS, nc=1, hcg_dtype=F32,
                               where=lambda b, c: (b * nc + c, 0))
        kw_p = dict(rows=CHUNK, n_valid=CHUNK, nc=nc, zero_init=True)
        kw_s = dict(rows=SAMPLE_ROWS, n_valid=dec_seq, nc=1, zero_init=False)
        (hcg_p, *st_p), (hcg_s, *st_s) = _mlstm_pair((specs_p, kw_p), (specs_s, kw_s), st_p, st_s,
                                                     nseq_p=batch, nc_p=nc)

        xp = merge(xp, a_p, hcg_p, pb_p, pe_p)
        xs = merge(xs, a_s, hcg_s, pb_s, pe_s)

        pool_p.append(pa_p.reshape(batch, seq, POOL_WIDTH)[:, seq - POOL_HIST:])
        conv_p.append(tail_p[:, CONV_HALO - (CONV_W - 1):])
        sa = pa_s.reshape(dec_batch, SAMPLE_ROWS, POOL_WIDTH)
        sb = pb_s.reshape(dec_batch, SAMPLE_ROWS, PB_COLS)
        pool_s.append(jnp.concatenate([state_pool[l][:, dec_seq:], sa[:, :dec_seq]], axis=1))
        conv_s.append(sb[:, dec_seq - (CONV_W - 1):dec_seq, PB_Q:PB_Q + 2 * MLSTM_WIDTH])

    y_prompt = xp.reshape(batch, seq, D_MODEL)
    y_sample = xs.reshape(dec_batch, SAMPLE_ROWS, D_MODEL)[:, :dec_seq]
    c_p, n_p, m_p = st_p
    c_s, n_s, m_s = st_s
    return (y_prompt, y_sample, jnp.stack(pool_p), jnp.stack(pool_s), jnp.stack(conv_p),
            jnp.stack(conv_s), c_p, c_s, n_p[:, :, :, 0, :], n_s[:, :, :, 0, :],
            m_p[:, :, :, 0, 0], m_s[:, :, :, 0, 0])
```

```python
import functools

import jax
import jax.numpy as jnp
from jax import lax
from jax.experimental import pallas as pl
from jax.experimental.pallas import tpu as pltpu

F32 = jnp.float32
BF16 = jnp.bfloat16

D_MODEL = 1024
DEPTH = 4
POOL_WINDOWS = (2, 4, 8, 16)
POOL_WIDTH = D_MODEL
POOL_GROUP_DIM = POOL_WIDTH // len(POOL_WINDOWS)
POOL_HIST = max(POOL_WINDOWS) - 1
MLSTM_WIDTH = 2 * D_MODEL
N_HEADS = 4
HEAD_DIM = MLSTM_WIDTH // N_HEADS
CONV_W = 4
PLE_DIM = 256
CHUNK = 128
EPS = 1e-6
PAST_LEN = 16384

IF_OFF = 2 * POOL_WIDTH + 5 * MLSTM_WIDTH
N_GATES = 2 * N_HEADS
W_MAIN_COLS = IF_OFF
W_TAIL_COLS = 2 * D_MODEL

PB_Q, PB_K, PB_V, PB_O, PB_MZ = (i * MLSTM_WIDTH for i in range(5))
PB_GATE_A = 5 * MLSTM_WIDTH
PB_GATE_B = PB_GATE_A + D_MODEL
PB_POOL_Z = PB_GATE_B + D_MODEL
PB_COLS = PB_POOL_Z + POOL_WIDTH

LANES = 128
SUBLANES = 8
POOL_HALO = 16
CONV_HALO = SUBLANES
SAMPLE_ROWS = SUBLANES
VMEM_LIMIT = 56 * 1024 * 1024
N_STATE = 3


def _sigmoid(x):
    return 0.5 * jnp.tanh(0.5 * x) + 0.5


def _silu(x):
    hx = 0.5 * x
    return hx * jnp.tanh(hx) + hx


def _log_sigmoid(x):
    return jnp.minimum(x, 0.0) - jnp.log1p(jnp.exp(-jnp.abs(x)))


def _rms(x, w):
    return x * lax.rsqrt(jnp.mean(x * x, axis=-1, keepdims=True) + EPS) * w


def _dot(a, b):
    return jnp.dot(a, b, preferred_element_type=F32)


def _params(*sem):
    return pltpu.CompilerParams(dimension_semantics=sem, vmem_limit_bytes=VMEM_LIMIT)


def _dot_t(a, b):
    return lax.dot_general(a, b, (((1,), (1,)), ((), ())), preferred_element_type=F32)


def _inproj_tiles(tn):
    u = D_MODEL // tn
    n_main = W_MAIN_COLS // tn
    n_total = n_main + W_TAIL_COLS // tn
    n_qk = 2 * MLSTM_WIDTH // tn

    def pa_idx(n):
        return jnp.minimum(n, u - 1)

    def pb_idx(n):
        return jnp.where(n < u, PB_POOL_Z // tn,
                         jnp.where(n < 2 * u, PB_POOL_Z // tn + n - u, n - 2 * u))

    def tail_idx(n):
        return jnp.clip(n - 2 * u, 0, n_qk - 1)

    return u, n_main, n_total, n_qk, pa_idx, pb_idx, tail_idx


def _norm_gates(x, nw, wih, wil, bif):
    h = _rms(x, nw)
    hh = h.astype(BF16)
    hl = (h - hh.astype(F32)).astype(BF16)
    return hh, _dot_t(hh, wih) + _dot_t(hl, wih) + _dot_t(hh, wil) + bif


def _prenorm_kernel(x_ref, nw_ref, wih_ref, wil_ref, bif_ref, h_ref, gates_ref):
    h_ref[...], gates_ref[...] = _norm_gates(x_ref[...], nw_ref[...], wih_ref[...], wil_ref[...],
                                             bif_ref[...])


def _norm_specs(l):
    return [
        pl.BlockSpec((None, 1, D_MODEL), lambda i: (l, 0, 0)),
        pl.BlockSpec((None, LANES, D_MODEL), lambda i: (l, 0, 0)),
        pl.BlockSpec((None, LANES, D_MODEL), lambda i: (l, 0, 0)),
        pl.BlockSpec((None, 1, LANES), lambda i: (l, 0, 0)),
    ]


def _prenorm(x, norm, *, l, tm):
    t = x.shape[0]
    return pl.pallas_call(
        _prenorm_kernel,
        grid=(t // tm,),
        in_specs=[pl.BlockSpec((tm, D_MODEL), lambda i: (i, 0))] + _norm_specs(l),
        out_specs=[pl.BlockSpec((tm, D_MODEL), lambda i: (i, 0)),
                   pl.BlockSpec((tm, LANES), lambda i: (i, 0))],
        out_shape=[jax.ShapeDtypeStruct((t, D_MODEL), BF16), jax.ShapeDtypeStruct((t, LANES), F32)],
        compiler_params=_params("parallel"),
        name="prenorm",
    )(x, *norm)


def _inproj_kernel(*refs, tn, seq_rows):
    if seq_rows:
        h_ref, w_ref, wt_ref, pa_ref, pb_ref, tail_ref = refs
    else:
        h_ref, w_ref, wt_ref, pa_ref, pb_ref = refs
    u, n_main, _, n_qk, _, _, _ = _inproj_tiles(tn)
    n = pl.program_id(1)

    @pl.when(n < u)
    def _():
        pa_ref[...] = _dot_t(h_ref[...], w_ref[...].astype(BF16))

    in_main = (n >= u) & (n < n_main)
    if seq_rows:
        is_qk = (n >= 2 * u) & (n < 2 * u + n_qk)

        @pl.when(is_qk)
        def _():
            acc = _dot_t(h_ref[...], w_ref[...].astype(BF16))
            pb_ref[...] = acc.astype(pb_ref.dtype)
            for s in range(acc.shape[0] // seq_rows):
                tail_ref[s] = acc[(s + 1) * seq_rows - CONV_HALO:(s + 1) * seq_rows, :]

        in_main = in_main & jnp.logical_not(is_qk)

    @pl.when(in_main)
    def _():
        pb_ref[...] = _dot_t(h_ref[...], w_ref[...].astype(BF16)).astype(pb_ref.dtype)

    @pl.when(n >= n_main)
    def _():
        pb_ref[...] = _dot_t(h_ref[...], wt_ref[...].astype(BF16)).astype(pb_ref.dtype)


def _inproj(h, w_main_t, w_tail_t, *, l, tm, tn, pb_dtype, seq_rows):
    t = h.shape[0]
    _, n_main, n_total, _, pa_idx, pb_idx, tail_idx = _inproj_tiles(tn)
    out_specs = [
        pl.BlockSpec((tm, tn), lambda m, n: (m, pa_idx(n))),
        pl.BlockSpec((tm, tn), lambda m, n: (m, pb_idx(n))),
    ]
    out_shape = [
        jax.ShapeDtypeStruct((t, POOL_WIDTH), F32),
        jax.ShapeDtypeStruct((t, PB_COLS), pb_dtype),
    ]
    if seq_rows:
        out_specs.append(pl.BlockSpec((tm // seq_rows, CONV_HALO, tn), lambda m, n: (m, 0, tail_idx(n))))
        out_shape.append(jax.ShapeDtypeStruct((t // seq_rows, CONV_HALO, 2 * MLSTM_WIDTH), F32))
    return pl.pallas_call(
        functools.partial(_inproj_kernel, tn=tn, seq_rows=seq_rows),
        grid=(t // tm, n_total),
        in_specs=[
            pl.BlockSpec((tm, D_MODEL), lambda m, n: (m, 0)),
            pl.BlockSpec((None, tn, D_MODEL), lambda m, n: (l, jnp.minimum(n, n_main - 1), 0)),
            pl.BlockSpec((None, tn, D_MODEL), lambda m, n: (l, jnp.maximum(n - n_main, 0), 0)),
        ],
        out_specs=out_specs,
        out_shape=out_shape,
        compiler_params=_params("parallel", "arbitrary"),
        name="inproj",
    )(h, w_main_t, w_tail_t)


def _pool_kernel(*refs, nb, rows, seq_tiles, pos0):
    if seq_tiles > 1:
        u_ref, z_ref, halo_ref, hist_ref, pw_ref, ps_ref, wd_ref, a_ref, xs_ref = refs
        st = pl.program_id(0) % seq_tiles
        halo = jnp.where(st == 0, hist_ref[...], halo_ref[...].reshape(1, POOL_HALO, POOL_WIDTH))
        tile_pos0 = pos0 + st * rows
    else:
        u_ref, z_ref, hist_ref, pw_ref, ps_ref, wd_ref, a_ref, xs_ref = refs
        halo = hist_ref[...]
        tile_pos0 = pos0
    xs_ref[:, 0:POOL_HALO, :] = halo
    xs_ref[:, POOL_HALO:, :] = u_ref[...].reshape(nb, rows, POOL_WIDTH)
    ext = POOL_HALO + rows
    pos = tile_pos0 + lax.broadcasted_iota(jnp.int32, (nb, rows, POOL_GROUP_DIM), 1)
    mixed = []
    for g, win in enumerate(POOL_WINDOWS):
        cols = slice(g * POOL_GROUP_DIM, (g + 1) * POOL_GROUP_DIM)
        acc = xs_ref[:, :, cols].reshape(nb * ext, POOL_GROUP_DIM)
        span = 1
        while span < win:
            acc = acc + pltpu.roll(acc, span, axis=0)
            span *= 2
        acc = acc.reshape(nb, ext, POOL_GROUP_DIM)[:, POOL_HALO:, :]
        cnt = jnp.minimum(pos + 1, win).astype(F32)
        pooled = (acc / cnt - xs_ref[:, POOL_HALO:, cols]).reshape(nb * rows, POOL_GROUP_DIM)
        mixed.append(_dot(pooled.astype(BF16), pw_ref[g]))
    mixed = jnp.concatenate(mixed, axis=-1) * ps_ref[...]
    act = mixed * _silu(z_ref[...].astype(F32))
    a_ref[...] = _dot(act.astype(BF16), wd_ref[...])


def _pool(pa, pb, hist, pw, ps, wd, *, l, hist_l, nb, rows, seq_tiles, pos0):
    t = pa.shape[0]
    tm = nb * rows
    in_specs = [
        pl.BlockSpec((tm, POOL_WIDTH), lambda i: (i, 0)),
        pl.BlockSpec((tm, POOL_WIDTH), lambda i: (i, PB_POOL_Z // POOL_WIDTH)),
    ]
    args = [pa, pb]
    if seq_tiles > 1:
        per = rows // POOL_HALO
        in_specs.append(pl.BlockSpec((POOL_HALO, POOL_WIDTH),
                                     lambda i: (jnp.maximum(i * per - 1, 0), 0)))
        args.append(pa)
        in_specs.append(pl.BlockSpec((None, 1, POOL_HALO, POOL_WIDTH),
                                     lambda i: (hist_l, i // seq_tiles, 0, 0)))
    else:
        in_specs.append(pl.BlockSpec((None, nb, POOL_HALO, POOL_WIDTH), lambda i: (hist_l, i, 0, 0)))
    args.append(hist)
    in_specs += [
        pl.BlockSpec((None, len(POOL_WINDOWS), POOL_GROUP_DIM, POOL_GROUP_DIM), lambda i: (l, 0, 0, 0)),
        pl.BlockSpec((None, 1, POOL_WIDTH), lambda i: (l, 0, 0)),
        pl.BlockSpec((None, POOL_WIDTH, D_MODEL), lambda i: (l, 0, 0)),
    ]
    args += [pw, ps, wd]
    return pl.pallas_call(
        functools.partial(_pool_kernel, nb=nb, rows=rows, seq_tiles=seq_tiles, pos0=pos0),
        grid=(t // tm,),
        in_specs=in_specs,
        out_specs=pl.BlockSpec((tm, D_MODEL), lambda i: (i, 0)),
        out_shape=jax.ShapeDtypeStruct((t, D_MODEL), F32),
        scratch_shapes=[pltpu.VMEM((nb, POOL_HALO + rows, POOL_WIDTH), F32)],
        compiler_params=_params("parallel"),
        name="pool",
    )(*args)


def _conv_silu(halo, x, w_ref, b_ref, wcols):
    z = jnp.concatenate([halo, x], axis=0)
    acc = z * w_ref[0:1, wcols]
    for j in range(1, CONV_W):
        acc = pltpu.roll(acc, 1, axis=0) + z * w_ref[j:j + 1, wcols]
    return _silu(acc[CONV_HALO:, :] + b_ref[:, wcols])


def _mlstm_stream(ins, outs, first, *, rows, n_valid, nc, zero_init):
    ins = list(ins)
    q_ref, k_ref, v_ref, o_ref, mz_ref = ins[:5]
    ins = ins[5:]
    if nc > 1:
        qh_ref, kh_ref = ins[:2]
        ins = ins[2:]
    hist_ref, g_ref, cw_ref, cb_ref, nw_ref = ins[:5]
    ins = ins[5:]
    if not zero_init:
        c0_ref, n0_ref, m0_ref = ins[:3]
    hcg_ref, c_ref, n_ref, m_ref = outs

    if zero_init:
        @pl.when(first)
        def _():
            c_ref[...] = jnp.zeros(c_ref.shape, F32)
            n_ref[...] = jnp.zeros(n_ref.shape, F32)
            m_ref[...] = jnp.zeros(m_ref.shape, F32)

    kq = slice(0, MLSTM_WIDTH)
    kk = slice(MLSTM_WIDTH, 2 * MLSTM_WIDTH)
    if nc > 1:
        lo = qh_ref.shape[0] - CONV_HALO
        q_halo = jnp.where(first, hist_ref[0, :, kq], qh_ref[...].astype(F32)[lo:, :])
        k_halo = jnp.where(first, hist_ref[0, :, kk], kh_ref[...].astype(F32)[lo:, :])
    else:
        q_halo = hist_ref[0, :, kq]
        k_halo = hist_ref[0, :, kk]

    row = lax.broadcasted_iota(jnp.int32, (rows, rows), 0)
    col = lax.broadcasted_iota(jnp.int32, (rows, rows), 1)
    causal = col <= row
    diag = col == row
    gates = g_ref[...]

    for h in range(N_HEADS):
        hs = slice(h * HEAD_DIM, (h + 1) * HEAD_DIM)
        ks = slice(MLSTM_WIDTH + h * HEAD_DIM, MLSTM_WIDTH + (h + 1) * HEAD_DIM)
        q = _conv_silu(q_halo[:, hs], q_ref[:, hs].astype(F32), cw_ref, cb_ref, hs)
        k = _conv_silu(k_halo[:, hs], k_ref[:, hs].astype(F32), cw_ref, cb_ref, ks) * (HEAD_DIM ** -0.5)
        vb = v_ref[:, hs].astype(BF16)

        i_col = gates[:, h:h + 1]
        ls_col = _log_sigmoid(gates[:, N_HEADS + h:N_HEADS + h + 1])
        i_row = jnp.sum(jnp.where(diag, i_col, 0.0), axis=0, keepdims=True)
        ls_row = jnp.sum(jnp.where(diag, ls_col, 0.0), axis=0, keepdims=True)
        b_col = jnp.sum(jnp.where(causal, ls_row, 0.0), axis=1, keepdims=True)
        b_row = jnp.sum(jnp.where(row <= col, ls_col, 0.0), axis=0, keepdims=True)

        if zero_init:
            c, n, m = c_ref[0, h], n_ref[0, h], m_ref[0, h][:, 0:1]
        else:
            c, n, m = c0_ref[0, h], n0_ref[0, h], m0_ref[0, h]

        log_d = jnp.where(causal, b_col - b_row + i_row, -jnp.inf)
        m_inter = b_col + m
        m_t = jnp.maximum(m_inter, jnp.max(log_d, axis=-1, keepdims=True))
        d = jnp.exp(log_d - m_t)
        qb, kb = q.astype(BF16), k.astype(BF16)
        s = lax.dot_general(qb, kb, (((1,), (1,)), ((), ())), preferred_element_type=F32) * d
        g = jnp.exp(m_inter - m_t)
        num = _dot(s.astype(BF16), vb) + g * _dot(qb, c.astype(BF16))
        den = jnp.sum(s, axis=-1, keepdims=True) + g * jnp.sum(q * n, axis=-1, keepdims=True)
        hh = num / jnp.maximum(jnp.abs(den), jnp.exp(-m_t))

        b_last = b_col[n_valid - 1:n_valid, :]
        m_new = m_t[n_valid - 1:n_valid, :]
        decay = jnp.exp(b_last + m - m_new)
        w_s = jnp.exp(b_last - b_col + i_col - m_new)
        kw = k * w_s
        if n_valid < rows:
            tok = lax.broadcasted_iota(jnp.int32, (rows, HEAD_DIM), 0)
            kw = jnp.where(tok < n_valid, kw, 0.0)
        c_ref[0, h] = decay * c + lax.dot_general(kw.astype(BF16), vb, (((0,), (0,)), ((), ())),
                                                  preferred_element_type=F32)
        n_ref[0, h] = decay * n + jnp.sum(kw, axis=0, keepdims=True)
        m_ref[0, h] = jnp.broadcast_to(m_new, (1, LANES))

        hc = hh * _sigmoid(o_ref[:, hs].astype(F32))
        y = _rms(hc, nw_ref[:, hs])
        hcg_ref[:, hs] = (y * _silu(mz_ref[:, hs].astype(F32))).astype(hcg_ref.dtype)


def _mlstm_specs(pb, gates, hist8, cw, cb, nw, state, *, l, hist_l, nseq, rows, nc, hcg_dtype, where):
    t = pb.shape[0]
    mw, hb = MLSTM_WIDTH, HEAD_DIM

    def blk(b, c):
        s, ch = where(b, c)
        return s * nc + ch

    def seq(b, c):
        return where(b, c)[0]

    def tok(off):
        return pl.BlockSpec((rows, mw), lambda b, c: (blk(b, c), off // mw))

    in_specs = [tok(PB_Q), tok(PB_K), tok(PB_V), tok(PB_O), tok(PB_MZ)]
    args = [pb] * 5
    if nc > 1:
        halo_rows = CONV_HALO * (4 // pb.dtype.itemsize)
        per = rows // halo_rows

        def halo(off):
            return pl.BlockSpec((halo_rows, mw),
                                lambda b, c: (jnp.maximum(blk(b, c) * per - 1, 0), off // mw))

        in_specs += [halo(PB_Q), halo(PB_K)]
        args += [pb, pb]
    in_specs += [
        pl.BlockSpec((None, 1, CONV_HALO, 2 * mw), lambda b, c: (hist_l, seq(b, c), 0, 0)),
        pl.BlockSpec((rows, LANES), lambda b, c: (blk(b, c), 0)),
        pl.BlockSpec((None, CONV_W, 2 * mw), lambda b, c: (l, 0, 0)),
        pl.BlockSpec((None, 1, 2 * mw), lambda b, c: (l, 0, 0)),
        pl.BlockSpec((None, 1, mw), lambda b, c: (l, 0, 0)),
    ]
    args += [hist8, gates, cw, cb, nw]
    c_spec = pl.BlockSpec((None, 1, N_HEADS, hb, hb), lambda b, c: (l, seq(b, c), 0, 0, 0))
    n_spec = pl.BlockSpec((None, 1, N_HEADS, 1, hb), lambda b, c: (l, seq(b, c), 0, 0, 0))
    if state is not None:
        in_specs += [c_spec, n_spec,
                     pl.BlockSpec((None, 1, N_HEADS, 1, 1), lambda b, c: (l, seq(b, c), 0, 0, 0))]
        args += list(state)
    out_specs = [
        pl.BlockSpec((rows, mw), lambda b, c: (blk(b, c), 0)),
        c_spec,
        n_spec,
        pl.BlockSpec((None, 1, N_HEADS, 1, LANES), lambda b, c: (l, seq(b, c), 0, 0, 0)),
    ]
    out_shape = [
        jax.ShapeDtypeStruct((t, mw), hcg_dtype),
        jax.ShapeDtypeStruct((DEPTH, nseq, N_HEADS, hb, hb), F32),
        jax.ShapeDtypeStruct((DEPTH, nseq, N_HEADS, 1, hb), F32),
        jax.ShapeDtypeStruct((DEPTH, nseq, N_HEADS, 1, LANES), F32),
    ]
    return in_specs, args, out_specs, out_shape


def _mlstm_pair_kernel(*refs, n_in, n_alias, kw_p, kw_s):
    n_out = 1 + N_STATE
    ins_p = refs[:n_in[0]]
    ins_s = refs[n_in[0]:n_in[0] + n_in[1]]
    outs = refs[n_in[0] + n_in[1] + n_alias:]
    first = pl.program_id(1) == 0
    _mlstm_stream(ins_p, outs[:n_out], first, **kw_p)
    _mlstm_stream(ins_s, outs[n_out:2 * n_out], True, **kw_s)


def _mlstm_pair(prompt, sample, prev_p, prev_s, *, nseq_p, nc_p):
    (in_p, args_p, out_p, shape_p), kw_p = prompt
    (in_s, args_s, out_s, shape_s), kw_s = sample
    in_specs, args = in_p + in_s, args_p + args_s
    n_out = 1 + N_STATE
    aliases = {}
    for base, prev in ((0, prev_p), (n_out, prev_s)):
        if prev is not None:
            for j, p in enumerate(prev):
                aliases[len(args)] = base + 1 + j
                in_specs.append(pl.BlockSpec(memory_space=pl.ANY))
                args.append(p)
    outs = pl.pallas_call(
        functools.partial(_mlstm_pair_kernel, n_in=(len(in_p), len(in_s)), n_alias=len(aliases),
                          kw_p=kw_p, kw_s=kw_s),
        grid=(nseq_p, nc_p),
        in_specs=in_specs,
        out_specs=out_p + out_s,
        out_shape=shape_p + shape_s,
        input_output_aliases=aliases,
        compiler_params=_params("parallel", "arbitrary"),
        name="mlstm",
    )(*args)
    return outs[:n_out], outs[n_out:]


def _merge_kernel(*refs, final):
    (x_ref, a_ref, hcg_ref, ga_ref, gb_ref, pe_ref, wmd_ref, wo_ref, wple_ref, pnw_ref,
     wpg_ref) = refs[:11]
    bm = _dot(hcg_ref[...].astype(BF16), wmd_ref[...])
    mix = (_sigmoid(ga_ref[...].astype(F32)) * a_ref[...]
           + _sigmoid(gb_ref[...].astype(F32)) * bm)
    x1 = x_ref[...] + _dot(mix.astype(BF16), wo_ref[...])
    r = _rms(x1, pnw_ref[...])
    gate = _sigmoid(_dot(r.astype(BF16), wpg_ref[...]))
    x2 = x1 + _dot(pe_ref[...].astype(BF16), wple_ref[...]) * gate
    if final:
        fw_ref, out_ref = refs[11:]
        out_ref[...] = _rms(x2, fw_ref[...])
    else:
        nw_ref, wih_ref, wil_ref, bif_ref, out_ref, h_ref, gates_ref = refs[11:]
        out_ref[...] = x2
        h_ref[...], gates_ref[...] = _norm_gates(x2, nw_ref[...], wih_ref[...], wil_ref[...],
                                                 bif_ref[...])


def _merge(x, a, hcg, pb, pe, wmd, wo, wple, pnw, wpg, fw, norm, *, l, tm):
    t = x.shape[0]
    final = l == DEPTH - 1

    def lw(*shape):
        return pl.BlockSpec((None,) + shape, lambda i: (l, 0, 0))

    row_spec = pl.BlockSpec((tm, D_MODEL), lambda i: (i, 0))
    if final:
        tail_specs, tail_args = [pl.BlockSpec((1, D_MODEL), lambda i: (0, 0))], [fw]
        out_specs = row_spec
        out_shape = jax.ShapeDtypeStruct((t, D_MODEL), F32)
    else:
        tail_specs, tail_args = _norm_specs(l + 1), list(norm)
        out_specs = [row_spec, row_spec, pl.BlockSpec((tm, LANES), lambda i: (i, 0))]
        out_shape = [jax.ShapeDtypeStruct((t, D_MODEL), F32), jax.ShapeDtypeStruct((t, D_MODEL), BF16),
                     jax.ShapeDtypeStruct((t, LANES), F32)]
    return pl.pallas_call(
        functools.partial(_merge_kernel, final=final),
        grid=(t // tm,),
        in_specs=[
            pl.BlockSpec((tm, D_MODEL), lambda i: (i, 0)),
            pl.BlockSpec((tm, D_MODEL), lambda i: (i, 0)),
            pl.BlockSpec((tm, MLSTM_WIDTH), lambda i: (i, 0)),
            pl.BlockSpec((tm, D_MODEL), lambda i: (i, PB_GATE_A // D_MODEL)),
            pl.BlockSpec((tm, D_MODEL), lambda i: (i, PB_GATE_B // D_MODEL)),
            pl.BlockSpec((None, tm, PLE_DIM), lambda i: (l, i, 0)),
            lw(MLSTM_WIDTH, D_MODEL),
            lw(D_MODEL, D_MODEL),
            lw(PLE_DIM, D_MODEL),
            lw(1, D_MODEL),
            lw(D_MODEL, D_MODEL),
        ] + tail_specs,
        out_specs=out_specs,
        out_shape=out_shape,
        compiler_params=_params("parallel"),
        name="merge",
    )(x, a, hcg, pb, pb, pe, wmd, wo, wple, pnw, wpg, *tail_args)


def kernel(x_prompt, x_sample, state_pool, state_conv, state_mlstm_C, state_mlstm_n, state_mlstm_m,
           p_prompt, p_sample, norm_w, w_in, b_if, conv_w, conv_b, pool_w, pool_scale, w_pool_down,
           mlstm_norm_w, w_mlstm_down, w_out, w_ple, ple_norm_w, w_ple_gate, final_norm_w):
    batch, seq, _ = x_prompt.shape
    dec_batch, dec_seq, _ = x_sample.shape
    nc = seq // CHUNK
    assert seq % CHUNK == 0 and CONV_W - 1 <= dec_seq <= SAMPLE_ROWS
    assert dec_batch == batch * nc
    pad_rows = SAMPLE_ROWS - dec_seq
    tp = batch * seq
    ts = dec_batch * SAMPLE_ROWS

    w_in_t = jnp.swapaxes(w_in, 1, 2)
    w_tail_t = w_in_t[:, IF_OFF + N_GATES:, :]
    w_if = jnp.pad(w_in_t[:, IF_OFF:IF_OFF + N_GATES, :], ((0, 0), (0, LANES - N_GATES), (0, 0)))
    w_if_hi = w_if.astype(BF16)
    w_if_lo = (w_if - w_if_hi.astype(F32)).astype(BF16)
    b_if_p = jnp.pad(b_if, ((0, 0), (0, LANES - N_GATES)))[:, None, :]
    pool_w_b = pool_w.astype(BF16)
    w_pool_down_b = w_pool_down.astype(BF16)
    w_mlstm_down_b = w_mlstm_down.astype(BF16)
    w_out_b = w_out.astype(BF16)
    w_ple_b = w_ple.astype(BF16)
    w_ple_gate_b = w_ple_gate.astype(BF16)
    norm_w3, conv_b3, pool_scale3 = norm_w[:, None, :], conv_b[:, None, :], pool_scale[:, None, :]
    mlstm_norm_w3, ple_norm_w3 = mlstm_norm_w[:, None, :], ple_norm_w[:, None, :]

    xp = x_prompt.reshape(tp, D_MODEL)
    xs = jnp.pad(x_sample, ((0, 0), (0, pad_rows), (0, 0))).reshape(ts, D_MODEL)
    pe_s = jnp.pad(p_sample, ((0, 0), (0, 0), (0, pad_rows), (0, 0))).reshape(DEPTH, ts, PLE_DIM)
    pe_p = p_prompt.reshape(DEPTH, tp, PLE_DIM)
    hist16_s = jnp.pad(state_pool, ((0, 0), (0, 0), (POOL_HALO - POOL_HIST, 0), (0, 0)))
    hist8_s = jnp.pad(state_conv, ((0, 0), (0, 0), (CONV_HALO - (CONV_W - 1), 0), (0, 0)))
    hist16_p = jnp.zeros((1, batch, POOL_HALO, POOL_WIDTH), F32)
    hist8_p = jnp.zeros((1, batch, CONV_HALO, 2 * MLSTM_WIDTH), F32)
    state_s = (state_mlstm_C, state_mlstm_n[:, :, :, None, :], state_mlstm_m[:, :, :, None, None])
    fw = final_norm_w[None, :]

    pool_tile = 512
    seqs_per_tile = 2 if batch % 2 == 0 else 1
    tn_p = 256
    tn_s = 1024
    tm_merge = 512
    norm = (norm_w3, w_if_hi, w_if_lo, b_if_p)
    pool_p, pool_s, conv_p, conv_s = [], [], [], []
    st_p = st_s = None
    hp, gates_p = _prenorm(xp, norm, l=0, tm=tm_merge)
    hs, gates_s = _prenorm(xs, norm, l=0, tm=tm_merge)
    for l in range(DEPTH):
        def project(h, *, tm, tn, dtype, seq_rows):
            return _inproj(h, w_in_t, w_tail_t, l=l, tm=tm, tn=tn, pb_dtype=dtype, seq_rows=seq_rows)

        def pool(pa, pb, hist16, **kw):
            return _pool(pa, pb, hist16, pool_w_b, pool_scale3, w_pool_down_b, l=l, **kw)

        def merge(x, a, hcg, pb, pe):
            return _merge(x, a, hcg, pb, pe, w_mlstm_down_b, w_out_b, w_ple_b, ple_norm_w3, w_ple_gate_b,
                          fw, norm, l=l, tm=tm_merge)

        pa_p, pb_p, tail_p = project(hp, tm=seqs_per_tile * seq, tn=tn_p, dtype=BF16, seq_rows=seq)
        pa_s, pb_s = project(hs, tm=ts, tn=tn_s, dtype=F32, seq_rows=None)
        a_p = pool(pa_p, pb_p, hist16_p, hist_l=0, nb=1, rows=pool_tile, seq_tiles=seq // pool_tile,
                   pos0=0)
        a_s = pool(pa_s, pb_s, hist16_s, hist_l=l, nb=64, rows=SAMPLE_ROWS, seq_tiles=1, pos0=PAST_LEN)

        specs_p = _mlstm_specs(pb_p, gates_p, hist8_p, conv_w, conv_b3, mlstm_norm_w3, None, l=l,
                               hist_l=0, nseq=batch, rows=CHUNK, nc=nc, hcg_dtype=BF16,
                               where=lambda b, c: (b, c))
        specs_s = _mlstm_specs(pb_s, gates_s, hist8_s, conv_w, conv_b3, mlstm_norm_w3, state_s, l=l,
                               hist_l=l, nseq=dec_batch, rows=SAMPLE_ROWS, nc=1, hcg_dtype=F32,
                               where=lambda b, c: (b * nc + c, 0))
        kw_p = dict(rows=CHUNK, n_valid=CHUNK, nc=nc, zero_init=True)
        kw_s = dict(rows=SAMPLE_ROWS, n_valid=dec_seq, nc=1, zero_init=False)
        (hcg_p, *st_p), (hcg_s, *st_s) = _mlstm_pair((specs_p, kw_p), (specs_s, kw_s), st_p, st_s,
                                                     nseq_p=batch, nc_p=nc)

        if l < DEPTH - 1:
            xp, hp, gates_p = merge(xp, a_p, hcg_p, pb_p, pe_p)
            xs, hs, gates_s = merge(xs, a_s, hcg_s, pb_s, pe_s)
        else:
            xp = merge(xp, a_p, hcg_p, pb_p, pe_p)
            xs = merge(xs, a_s, hcg_s, pb_s, pe_s)

        pool_p.append(pa_p.reshape(batch, seq, POOL_WIDTH)[:, seq - POOL_HIST:])
        conv_p.append(tail_p[:, CONV_HALO - (CONV_W - 1):])
        sa = pa_s.reshape(dec_batch, SAMPLE_ROWS, POOL_WIDTH)
        sb = pb_s.reshape(dec_batch, SAMPLE_ROWS, PB_COLS)
        pool_s.append(jnp.concatenate([state_pool[l][:, dec_seq:], sa[:, :dec_seq]], axis=1))
        conv_s.append(sb[:, dec_seq - (CONV_W - 1):dec_seq, PB_Q:PB_Q + 2 * MLSTM_WIDTH])

    y_prompt = xp.reshape(batch, seq, D_MODEL)
    y_sample = xs.reshape(dec_batch, SAMPLE_ROWS, D_MODEL)[:, :dec_seq]
    c_p, n_p, m_p = st_p
    c_s, n_s, m_s = st_s
    return (y_prompt, y_sample, jnp.stack(pool_p), jnp.stack(pool_s), jnp.stack(conv_p),
            jnp.stack(conv_s), c_p, c_s, n_p[:, :, :, 0, :], n_s[:, :, :, 0, :],
            m_p[:, :, :, 0, 0], m_s[:, :, :, 0, 0])
```

```python
import functools

import jax
import jax.numpy as jnp
from jax import lax
from jax.experimental import pallas as pl
from jax.experimental.pallas import tpu as pltpu

F32 = jnp.float32
BF16 = jnp.bfloat16

D_MODEL = 1024
DEPTH = 4
POOL_WINDOWS = (2, 4, 8, 16)
POOL_WIDTH = D_MODEL
POOL_GROUP_DIM = POOL_WIDTH // len(POOL_WINDOWS)
POOL_HIST = max(POOL_WINDOWS) - 1
MLSTM_WIDTH = 2 * D_MODEL
N_HEADS = 4
HEAD_DIM = MLSTM_WIDTH // N_HEADS
CONV_W = 4
PLE_DIM = 256
CHUNK = 128
EPS = 1e-6
PAST_LEN = 16384

IF_OFF = 2 * POOL_WIDTH + 5 * MLSTM_WIDTH
N_GATES = 2 * N_HEADS
W_MAIN_COLS = IF_OFF
W_TAIL_COLS = 2 * D_MODEL

PB_Q, PB_K, PB_V, PB_O, PB_MZ = (i * MLSTM_WIDTH for i in range(5))
PB_GATE_A = 5 * MLSTM_WIDTH
PB_GATE_B = PB_GATE_A + D_MODEL
PB_POOL_Z = PB_GATE_B + D_MODEL
PB_COLS = PB_POOL_Z + POOL_WIDTH

LANES = 128
SUBLANES = 8
POOL_HALO = 16
CONV_HALO = SUBLANES
SAMPLE_ROWS = SUBLANES
VMEM_LIMIT = 56 * 1024 * 1024
N_STATE = 3


def _sigmoid(x):
    return 0.5 * jnp.tanh(0.5 * x) + 0.5


def _silu(x):
    hx = 0.5 * x
    return hx * jnp.tanh(hx) + hx


def _log_sigmoid(x):
    return jnp.minimum(x, 0.0) - jnp.log1p(jnp.exp(-jnp.abs(x)))


def _rms(x, w):
    return x * lax.rsqrt(jnp.mean(x * x, axis=-1, keepdims=True) + EPS) * w


def _dot(a, b):
    return jnp.dot(a, b, preferred_element_type=F32)


def _params(*sem):
    return pltpu.CompilerParams(dimension_semantics=sem, vmem_limit_bytes=VMEM_LIMIT)


def _dot_t(a, b):
    return lax.dot_general(a, b, (((1,), (1,)), ((), ())), preferred_element_type=F32)


def _inproj_tiles(tn):
    u = D_MODEL // tn
    n_main = W_MAIN_COLS // tn
    n_total = n_main + W_TAIL_COLS // tn
    n_qk = 2 * MLSTM_WIDTH // tn

    def pa_idx(n):
        return jnp.minimum(n, u - 1)

    def pb_idx(n):
        return jnp.where(n < u, PB_POOL_Z // tn,
                         jnp.where(n < 2 * u, PB_POOL_Z // tn + n - u, n - 2 * u))

    def tail_idx(n):
        return jnp.clip(n - 2 * u, 0, n_qk - 1)

    return u, n_main, n_total, n_qk, pa_idx, pb_idx, tail_idx


def _norm_gates(x, nw, wih, wil, bif):
    h = _rms(x, nw)
    hh = h.astype(BF16)
    hl = (h - hh.astype(F32)).astype(BF16)
    return hh, _dot_t(hh, wih) + _dot_t(hl, wih) + _dot_t(hh, wil) + bif


def _prenorm_kernel(x_ref, nw_ref, wih_ref, wil_ref, bif_ref, h_ref, gates_ref):
    h_ref[...], gates_ref[...] = _norm_gates(x_ref[...], nw_ref[...], wih_ref[...], wil_ref[...],
                                             bif_ref[...])


def _norm_specs(l):
    return [
        pl.BlockSpec((None, 1, D_MODEL), lambda i: (l, 0, 0)),
        pl.BlockSpec((None, LANES, D_MODEL), lambda i: (l, 0, 0)),
        pl.BlockSpec((None, LANES, D_MODEL), lambda i: (l, 0, 0)),
        pl.BlockSpec((None, 1, LANES), lambda i: (l, 0, 0)),
    ]


def _prenorm(x, norm, *, l, tm):
    t = x.shape[0]
    return pl.pallas_call(
        _prenorm_kernel,
        grid=(t // tm,),
        in_specs=[pl.BlockSpec((tm, D_MODEL), lambda i: (i, 0))] + _norm_specs(l),
        out_specs=[pl.BlockSpec((tm, D_MODEL), lambda i: (i, 0)),
                   pl.BlockSpec((tm, LANES), lambda i: (i, 0))],
        out_shape=[jax.ShapeDtypeStruct((t, D_MODEL), BF16), jax.ShapeDtypeStruct((t, LANES), F32)],
        compiler_params=_params("parallel"),
        name="prenorm",
    )(x, *norm)


def _inproj_kernel(*refs, tn, seq_rows):
    if seq_rows:
        h_ref, w_ref, wt_ref, pa_ref, pb_ref, tail_ref = refs
    else:
        h_ref, w_ref, wt_ref, pa_ref, pb_ref = refs
    u, n_main, _, n_qk, _, _, _ = _inproj_tiles(tn)
    n = pl.program_id(1)

    @pl.when(n < u)
    def _():
        pa_ref[...] = _dot_t(h_ref[...], w_ref[...].astype(BF16))

    in_main = (n >= u) & (n < n_main)
    if seq_rows:
        is_qk = (n >= 2 * u) & (n < 2 * u + n_qk)

        @pl.when(is_qk)
        def _():
            acc = _dot_t(h_ref[...], w_ref[...].astype(BF16))
            pb_ref[...] = acc.astype(pb_ref.dtype)
            for s in range(acc.shape[0] // seq_rows):
                tail_ref[s] = acc[(s + 1) * seq_rows - CONV_HALO:(s + 1) * seq_rows, :]

        in_main = in_main & jnp.logical_not(is_qk)

    @pl.when(in_main)
    def _():
        pb_ref[...] = _dot_t(h_ref[...], w_ref[...].astype(BF16)).astype(pb_ref.dtype)

    @pl.when(n >= n_main)
    def _():
        pb_ref[...] = _dot_t(h_ref[...], wt_ref[...].astype(BF16)).astype(pb_ref.dtype)


def _inproj(h, w_main_t, w_tail_t, *, l, tm, tn, pb_dtype, seq_rows):
    t = h.shape[0]
    _, n_main, n_total, _, pa_idx, pb_idx, tail_idx = _inproj_tiles(tn)
    out_specs = [
        pl.BlockSpec((tm, tn), lambda m, n: (m, pa_idx(n))),
        pl.BlockSpec((tm, tn), lambda m, n: (m, pb_idx(n))),
    ]
    out_shape = [
        jax.ShapeDtypeStruct((t, POOL_WIDTH), F32),
        jax.ShapeDtypeStruct((t, PB_COLS), pb_dtype),
    ]
    if seq_rows:
        out_specs.append(pl.BlockSpec((tm // seq_rows, CONV_HALO, tn), lambda m, n: (m, 0, tail_idx(n))))
        out_shape.append(jax.ShapeDtypeStruct((t // seq_rows, CONV_HALO, 2 * MLSTM_WIDTH), F32))
    return pl.pallas_call(
        functools.partial(_inproj_kernel, tn=tn, seq_rows=seq_rows),
        grid=(t // tm, n_total),
        in_specs=[
            pl.BlockSpec((tm, D_MODEL), lambda m, n: (m, 0)),
            pl.BlockSpec((None, tn, D_MODEL), lambda m, n: (l, jnp.minimum(n, n_main - 1), 0)),
            pl.BlockSpec((None, tn, D_MODEL), lambda m, n: (l, jnp.maximum(n - n_main, 0), 0)),
        ],
        out_specs=out_specs,
        out_shape=out_shape,
        compiler_params=_params("parallel", "arbitrary"),
        name="inproj",
    )(h, w_main_t, w_tail_t)


def _pool_kernel(*refs, nb, rows, seq_tiles, pos0):
    if seq_tiles > 1:
        u_ref, z_ref, halo_ref, hist_ref, pw_ref, ps_ref, wd_ref, a_ref, xs_ref = refs
        st = pl.program_id(0) % seq_tiles
        halo = jnp.where(st == 0, hist_ref[...], halo_ref[...].reshape(1, POOL_HALO, POOL_WIDTH))
        tile_pos0 = pos0 + st * rows
    else:
        u_ref, z_ref, hist_ref, pw_ref, ps_ref, wd_ref, a_ref, xs_ref = refs
        halo = hist_ref[...]
        tile_pos0 = pos0
    xs_ref[:, 0:POOL_HALO, :] = halo
    xs_ref[:, POOL_HALO:, :] = u_ref[...].reshape(nb, rows, POOL_WIDTH)
    ext = POOL_HALO + rows
    pos = tile_pos0 + lax.broadcasted_iota(jnp.int32, (nb, rows, POOL_GROUP_DIM), 1)
    mixed = []
    for g, win in enumerate(POOL_WINDOWS):
        cols = slice(g * POOL_GROUP_DIM, (g + 1) * POOL_GROUP_DIM)
        acc = xs_ref[:, :, cols].reshape(nb * ext, POOL_GROUP_DIM)
        span = 1
        while span < win:
            acc = acc + pltpu.roll(acc, span, axis=0)
            span *= 2
        acc = acc.reshape(nb, ext, POOL_GROUP_DIM)[:, POOL_HALO:, :]
        cnt = jnp.minimum(pos + 1, win).astype(F32)
        pooled = (acc / cnt - xs_ref[:, POOL_HALO:, cols]).reshape(nb * rows, POOL_GROUP_DIM)
        mixed.append(_dot(pooled.astype(BF16), pw_ref[g]))
    mixed = jnp.concatenate(mixed, axis=-1) * ps_ref[...]
    act = mixed * _silu(z_ref[...].astype(F32))
    a_ref[...] = _dot(act.astype(BF16), wd_ref[...])


def _pool(pa, pb, hist, pw, ps, wd, *, l, hist_l, nb, rows, seq_tiles, pos0):
    t = pa.shape[0]
    tm = nb * rows
    in_specs = [
        pl.BlockSpec((tm, POOL_WIDTH), lambda i: (i, 0)),
        pl.BlockSpec((tm, POOL_WIDTH), lambda i: (i, PB_POOL_Z // POOL_WIDTH)),
    ]
    args = [pa, pb]
    if seq_tiles > 1:
        per = rows // POOL_HALO
        in_specs.append(pl.BlockSpec((POOL_HALO, POOL_WIDTH),
                                     lambda i: (jnp.maximum(i * per - 1, 0), 0)))
        args.append(pa)
        in_specs.append(pl.BlockSpec((None, 1, POOL_HALO, POOL_WIDTH),
                                     lambda i: (hist_l, i // seq_tiles, 0, 0)))
    else:
        in_specs.append(pl.BlockSpec((None, nb, POOL_HALO, POOL_WIDTH), lambda i: (hist_l, i, 0, 0)))
    args.append(hist)
    in_specs += [
        pl.BlockSpec((None, len(POOL_WINDOWS), POOL_GROUP_DIM, POOL_GROUP_DIM), lambda i: (l, 0, 0, 0)),
        pl.BlockSpec((None, 1, POOL_WIDTH), lambda i: (l, 0, 0)),
        pl.BlockSpec((None, POOL_WIDTH, D_MODEL), lambda i: (l, 0, 0)),
    ]
    args += [pw, ps, wd]
    return pl.pallas_call(
        functools.partial(_pool_kernel, nb=nb, rows=rows, seq_tiles=seq_tiles, pos0=pos0),
        grid=(t // tm,),
        in_specs=in_specs,
        out_specs=pl.BlockSpec((tm, D_MODEL), lambda i: (i, 0)),
        out_shape=jax.ShapeDtypeStruct((t, D_MODEL), F32),
        scratch_shapes=[pltpu.VMEM((nb, POOL_HALO + rows, POOL_WIDTH), F32)],
        compiler_params=_params("parallel"),
        name="pool",
    )(*args)


def _conv_silu(halo, x, w_ref, b_ref, wcols):
    z = jnp.concatenate([halo, x], axis=0)
    acc = z * w_ref[0:1, wcols]
    for j in range(1, CONV_W):
        acc = pltpu.roll(acc, 1, axis=0) + z * w_ref[j:j + 1, wcols]
    return _silu(acc[CONV_HALO:, :] + b_ref[:, wcols])


def _mlstm_stream(ins, outs, first, *, rows, n_valid, nc, zero_init):
    ins = list(ins)
    q_ref, k_ref, v_ref, o_ref, mz_ref = ins[:5]
    ins = ins[5:]
    if nc > 1:
        qh_ref, kh_ref = ins[:2]
        ins = ins[2:]
    hist_ref, g_ref, cw_ref, cb_ref, nw_ref = ins[:5]
    ins = ins[5:]
    if not zero_init:
        c0_ref, n0_ref, m0_ref = ins[:3]
    hcg_ref, c_ref, n_ref, m_ref = outs

    if zero_init:
        @pl.when(first)
        def _():
            c_ref[...] = jnp.zeros(c_ref.shape, F32)
            n_ref[...] = jnp.zeros(n_ref.shape, F32)
            m_ref[...] = jnp.zeros(m_ref.shape, F32)

    kq = slice(0, MLSTM_WIDTH)
    kk = slice(MLSTM_WIDTH, 2 * MLSTM_WIDTH)
    if nc > 1:
        lo = qh_ref.shape[0] - CONV_HALO
        q_halo = jnp.where(first, hist_ref[0, :, kq], qh_ref[...].astype(F32)[lo:, :])
        k_halo = jnp.where(first, hist_ref[0, :, kk], kh_ref[...].astype(F32)[lo:, :])
    else:
        q_halo = hist_ref[0, :, kq]
        k_halo = hist_ref[0, :, kk]

    row = lax.broadcasted_iota(jnp.int32, (rows, rows), 0)
    col = lax.broadcasted_iota(jnp.int32, (rows, rows), 1)
    causal = col <= row
    diag = col == row
    gates = g_ref[...]
    gates_t = gates.T if rows == LANES else None

    for h in range(N_HEADS):
        hs = slice(h * HEAD_DIM, (h + 1) * HEAD_DIM)
        ks = slice(MLSTM_WIDTH + h * HEAD_DIM, MLSTM_WIDTH + (h + 1) * HEAD_DIM)
        q = _conv_silu(q_halo[:, hs], q_ref[:, hs].astype(F32), cw_ref, cb_ref, hs)
        k = _conv_silu(k_halo[:, hs], k_ref[:, hs].astype(F32), cw_ref, cb_ref, ks) * (HEAD_DIM ** -0.5)
        vb = v_ref[:, hs].astype(BF16)

        i_col = gates[:, h:h + 1]
        ls_col = _log_sigmoid(gates[:, N_HEADS + h:N_HEADS + h + 1])
        if gates_t is not None:
            i_row = gates_t[h:h + 1, :]
            ls_row = _log_sigmoid(gates_t[N_HEADS + h:N_HEADS + h + 1, :])
        else:
            i_row = jnp.sum(jnp.where(diag, i_col, 0.0), axis=0, keepdims=True)
            ls_row = jnp.sum(jnp.where(diag, ls_col, 0.0), axis=0, keepdims=True)
        b_col = jnp.sum(jnp.where(causal, ls_row, 0.0), axis=1, keepdims=True)
        b_row = jnp.sum(jnp.where(row <= col, ls_col, 0.0), axis=0, keepdims=True)

        if zero_init:
            c, n, m = c_ref[0, h], n_ref[0, h], m_ref[0, h][:, 0:1]
        else:
            c, n, m = c0_ref[0, h], n0_ref[0, h], m0_ref[0, h]

        log_d = jnp.where(causal, b_col - b_row + i_row, -jnp.inf)
        m_inter = b_col + m
        m_t = jnp.maximum(m_inter, jnp.max(log_d, axis=-1, keepdims=True))
        d = jnp.exp(log_d - m_t)
        qb, kb = q.astype(BF16), k.astype(BF16)
        s = lax.dot_general(qb, kb, (((1,), (1,)), ((), ())), preferred_element_type=F32) * d
        g = jnp.exp(m_inter - m_t)
        num = _dot(s.astype(BF16), vb) + g * _dot(qb, c.astype(BF16))
        den = jnp.sum(s, axis=-1, keepdims=True) + g * jnp.sum(q * n, axis=-1, keepdims=True)
        hh = num / jnp.maximum(jnp.abs(den), jnp.exp(-m_t))

        b_last = b_col[n_valid - 1:n_valid, :]
        m_new = m_t[n_valid - 1:n_valid, :]
        decay = jnp.exp(b_last + m - m_new)
        w_s = jnp.exp(b_last - b_col + i_col - m_new)
        kw = k * w_s
        if n_valid < rows:
            tok = lax.broadcasted_iota(jnp.int32, (rows, HEAD_DIM), 0)
            kw = jnp.where(tok < n_valid, kw, 0.0)
        c_ref[0, h] = decay * c + lax.dot_general(kw.astype(BF16), vb, (((0,), (0,)), ((), ())),
                                                  preferred_element_type=F32)
        n_ref[0, h] = decay * n + jnp.sum(kw, axis=0, keepdims=True)
        m_ref[0, h] = jnp.broadcast_to(m_new, (1, LANES))

        hc = hh * _sigmoid(o_ref[:, hs].astype(F32))
        y = _rms(hc, nw_ref[:, hs])
        hcg_ref[:, hs] = (y * _silu(mz_ref[:, hs].astype(F32))).astype(hcg_ref.dtype)


def _mlstm_specs(pb, gates, hist8, cw, cb, nw, state, *, l, hist_l, nseq, rows, nc, hcg_dtype, where):
    t = pb.shape[0]
    mw, hb = MLSTM_WIDTH, HEAD_DIM

    def blk(b, c):
        s, ch = where(b, c)
        return s * nc + ch

    def seq(b, c):
        return where(b, c)[0]

    def tok(off):
        return pl.BlockSpec((rows, mw), lambda b, c: (blk(b, c), off // mw))

    in_specs = [tok(PB_Q), tok(PB_K), tok(PB_V), tok(PB_O), tok(PB_MZ)]
    args = [pb] * 5
    if nc > 1:
        halo_rows = CONV_HALO * (4 // pb.dtype.itemsize)
        per = rows // halo_rows

        def halo(off):
            return pl.BlockSpec((halo_rows, mw),
                                lambda b, c: (jnp.maximum(blk(b, c) * per - 1, 0), off // mw))

        in_specs += [halo(PB_Q), halo(PB_K)]
        args += [pb, pb]
    in_specs += [
        pl.BlockSpec((None, 1, CONV_HALO, 2 * mw), lambda b, c: (hist_l, seq(b, c), 0, 0)),
        pl.BlockSpec((rows, LANES), lambda b, c: (blk(b, c), 0)),
        pl.BlockSpec((None, CONV_W, 2 * mw), lambda b, c: (l, 0, 0)),
        pl.BlockSpec((None, 1, 2 * mw), lambda b, c: (l, 0, 0)),
        pl.BlockSpec((None, 1, mw), lambda b, c: (l, 0, 0)),
    ]
    args += [hist8, gates, cw, cb, nw]
    c_spec = pl.BlockSpec((None, 1, N_HEADS, hb, hb), lambda b, c: (l, seq(b, c), 0, 0, 0))
    n_spec = pl.BlockSpec((None, 1, N_HEADS, 1, hb), lambda b, c: (l, seq(b, c), 0, 0, 0))
    if state is not None:
        in_specs += [c_spec, n_spec,
                     pl.BlockSpec((None, 1, N_HEADS, 1, 1), lambda b, c: (l, seq(b, c), 0, 0, 0))]
        args += list(state)
    out_specs = [
        pl.BlockSpec((rows, mw), lambda b, c: (blk(b, c), 0)),
        c_spec,
        n_spec,
        pl.BlockSpec((None, 1, N_HEADS, 1, LANES), lambda b, c: (l, seq(b, c), 0, 0, 0)),
    ]
    out_shape = [
        jax.ShapeDtypeStruct((t, mw), hcg_dtype),
        jax.ShapeDtypeStruct((DEPTH, nseq, N_HEADS, hb, hb), F32),
        jax.ShapeDtypeStruct((DEPTH, nseq, N_HEADS, 1, hb), F32),
        jax.ShapeDtypeStruct((DEPTH, nseq, N_HEADS, 1, LANES), F32),
    ]
    return in_specs, args, out_specs, out_shape


def _mlstm_pair_kernel(*refs, n_in, n_alias, kw_p, kw_s):
    n_out = 1 + N_STATE
    ins_p = refs[:n_in[0]]
    ins_s = refs[n_in[0]:n_in[0] + n_in[1]]
    outs = refs[n_in[0] + n_in[1] + n_alias:]
    first = pl.program_id(1) == 0
    _mlstm_stream(ins_p, outs[:n_out], first, **kw_p)
    _mlstm_stream(ins_s, outs[n_out:2 * n_out], True, **kw_s)


def _mlstm_pair(prompt, sample, prev_p, prev_s, *, nseq_p, nc_p):
    (in_p, args_p, out_p, shape_p), kw_p = prompt
    (in_s, args_s, out_s, shape_s), kw_s = sample
    in_specs, args = in_p + in_s, args_p + args_s
    n_out = 1 + N_STATE
    aliases = {}
    for base, prev in ((0, prev_p), (n_out, prev_s)):
        if prev is not None:
            for j, p in enumerate(prev):
                aliases[len(args)] = base + 1 + j
                in_specs.append(pl.BlockSpec(memory_space=pl.ANY))
                args.append(p)
    outs = pl.pallas_call(
        functools.partial(_mlstm_pair_kernel, n_in=(len(in_p), len(in_s)), n_alias=len(aliases),
                          kw_p=kw_p, kw_s=kw_s),
        grid=(nseq_p, nc_p),
        in_specs=in_specs,
        out_specs=out_p + out_s,
        out_shape=shape_p + shape_s,
        input_output_aliases=aliases,
        compiler_params=_params("parallel", "arbitrary"),
        name="mlstm",
    )(*args)
    return outs[:n_out], outs[n_out:]


def _merge_kernel(*refs, final):
    (x_ref, a_ref, hcg_ref, ga_ref, gb_ref, pe_ref, wmd_ref, wo_ref, wple_ref, pnw_ref,
     wpg_ref) = refs[:11]
    bm = _dot(hcg_ref[...].astype(BF16), wmd_ref[...])
    mix = (_sigmoid(ga_ref[...].astype(F32)) * a_ref[...]
           + _sigmoid(gb_ref[...].astype(F32)) * bm)
    x1 = x_ref[...] + _dot(mix.astype(BF16), wo_ref[...])
    r = _rms(x1, pnw_ref[...])
    gate = _sigmoid(_dot(r.astype(BF16), wpg_ref[...]))
    x2 = x1 + _dot(pe_ref[...].astype(BF16), wple_ref[...]) * gate
    if final:
        fw_ref, out_ref = refs[11:]
        out_ref[...] = _rms(x2, fw_ref[...])
    else:
        nw_ref, wih_ref, wil_ref, bif_ref, out_ref, h_ref, gates_ref = refs[11:]
        out_ref[...] = x2
        h_ref[...], gates_ref[...] = _norm_gates(x2, nw_ref[...], wih_ref[...], wil_ref[...],
                                                 bif_ref[...])


def _merge(x, a, hcg, pb, pe, wmd, wo, wple, pnw, wpg, fw, norm, *, l, tm):
    t = x.shape[0]
    final = l == DEPTH - 1

    def lw(*shape):
        return pl.BlockSpec((None,) + shape, lambda i: (l, 0, 0))

    row_spec = pl.BlockSpec((tm, D_MODEL), lambda i: (i, 0))
    if final:
        tail_specs, tail_args = [pl.BlockSpec((1, D_MODEL), lambda i: (0, 0))], [fw]
        out_specs = row_spec
        out_shape = jax.ShapeDtypeStruct((t, D_MODEL), F32)
    else:
        tail_specs, tail_args = _norm_specs(l + 1), list(norm)
        out_specs = [row_spec, row_spec, pl.BlockSpec((tm, LANES), lambda i: (i, 0))]
        out_shape = [jax.ShapeDtypeStruct((t, D_MODEL), F32), jax.ShapeDtypeStruct((t, D_MODEL), BF16),
                     jax.ShapeDtypeStruct((t, LANES), F32)]
    return pl.pallas_call(
        functools.partial(_merge_kernel, final=final),
        grid=(t // tm,),
        in_specs=[
            pl.BlockSpec((tm, D_MODEL), lambda i: (i, 0)),
            pl.BlockSpec((tm, D_MODEL), lambda i: (i, 0)),
            pl.BlockSpec((tm, MLSTM_WIDTH), lambda i: (i, 0)),
            pl.BlockSpec((tm, D_MODEL), lambda i: (i, PB_GATE_A // D_MODEL)),
            pl.BlockSpec((tm, D_MODEL), lambda i: (i, PB_GATE_B // D_MODEL)),
            pl.BlockSpec((None, tm, PLE_DIM), lambda i: (l, i, 0)),
            lw(MLSTM_WIDTH, D_MODEL),
            lw(D_MODEL, D_MODEL),
            lw(PLE_DIM, D_MODEL),
            lw(1, D_MODEL),
            lw(D_MODEL, D_MODEL),
        ] + tail_specs,
        out_specs=out_specs,
        out_shape=out_shape,
        compiler_params=_params("parallel"),
        name="merge",
    )(x, a, hcg, pb, pb, pe, wmd, wo, wple, pnw, wpg, *tail_args)


def kernel(x_prompt, x_sample, state_pool, state_conv, state_mlstm_C, state_mlstm_n, state_mlstm_m,
           p_prompt, p_sample, norm_w, w_in, b_if, conv_w, conv_b, pool_w, pool_scale, w_pool_down,
           mlstm_norm_w, w_mlstm_down, w_out, w_ple, ple_norm_w, w_ple_gate, final_norm_w):
    batch, seq, _ = x_prompt.shape
    dec_batch, dec_seq, _ = x_sample.shape
    nc = seq // CHUNK
    assert seq % CHUNK == 0 and CONV_W - 1 <= dec_seq <= SAMPLE_ROWS
    assert dec_batch == batch * nc
    pad_rows = SAMPLE_ROWS - dec_seq
    tp = batch * seq
    ts = dec_batch * SAMPLE_ROWS

    w_in_t = jnp.swapaxes(w_in, 1, 2)
    w_tail_t = w_in_t[:, IF_OFF + N_GATES:, :]
    w_if = jnp.pad(w_in_t[:, IF_OFF:IF_OFF + N_GATES, :], ((0, 0), (0, LANES - N_GATES), (0, 0)))
    w_if_hi = w_if.astype(BF16)
    w_if_lo = (w_if - w_if_hi.astype(F32)).astype(BF16)
    b_if_p = jnp.pad(b_if, ((0, 0), (0, LANES - N_GATES)))[:, None, :]
    pool_w_b = pool_w.astype(BF16)
    w_pool_down_b = w_pool_down.astype(BF16)
    w_mlstm_down_b = w_mlstm_down.astype(BF16)
    w_out_b = w_out.astype(BF16)
    w_ple_b = w_ple.astype(BF16)
    w_ple_gate_b = w_ple_gate.astype(BF16)
    norm_w3, conv_b3, pool_scale3 = norm_w[:, None, :], conv_b[:, None, :], pool_scale[:, None, :]
    mlstm_norm_w3, ple_norm_w3 = mlstm_norm_w[:, None, :], ple_norm_w[:, None, :]

    xp = x_prompt.reshape(tp, D_MODEL)
    xs = jnp.pad(x_sample, ((0, 0), (0, pad_rows), (0, 0))).reshape(ts, D_MODEL)
    pe_s = jnp.pad(p_sample, ((0, 0), (0, 0), (0, pad_rows), (0, 0))).reshape(DEPTH, ts, PLE_DIM)
    pe_p = p_prompt.reshape(DEPTH, tp, PLE_DIM)
    hist16_s = jnp.pad(state_pool, ((0, 0), (0, 0), (POOL_HALO - POOL_HIST, 0), (0, 0)))
    hist8_s = jnp.pad(state_conv, ((0, 0), (0, 0), (CONV_HALO - (CONV_W - 1), 0), (0, 0)))
    hist16_p = jnp.zeros((1, batch, POOL_HALO, POOL_WIDTH), F32)
    hist8_p = jnp.zeros((1, batch, CONV_HALO, 2 * MLSTM_WIDTH), F32)
    state_s = (state_mlstm_C, state_mlstm_n[:, :, :, None, :], state_mlstm_m[:, :, :, None, None])
    fw = final_norm_w[None, :]

    pool_tile = 512
    seqs_per_tile = 2 if batch % 2 == 0 else 1
    tn_p = 256
    tn_s = 1024
    tm_merge = 512
    norm = (norm_w3, w_if_hi, w_if_lo, b_if_p)
    pool_p, pool_s, conv_p, conv_s = [], [], [], []
    st_p = st_s = None
    hp, gates_p = _prenorm(xp, norm, l=0, tm=tm_merge)
    hs, gates_s = _prenorm(xs, norm, l=0, tm=tm_merge)
    for l in range(DEPTH):
        def project(h, *, tm, tn, dtype, seq_rows):
            return _inproj(h, w_in_t, w_tail_t, l=l, tm=tm, tn=tn, pb_dtype=dtype, seq_rows=seq_rows)

        def pool(pa, pb, hist16, **kw):
            return _pool(pa, pb, hist16, pool_w_b, pool_scale3, w_pool_down_b, l=l, **kw)

        def merge(x, a, hcg, pb, pe):
            return _merge(x, a, hcg, pb, pe, w_mlstm_down_b, w_out_b, w_ple_b, ple_norm_w3, w_ple_gate_b,
                          fw, norm, l=l, tm=tm_merge)

        pa_p, pb_p, tail_p = project(hp, tm=seqs_per_tile * seq, tn=tn_p, dtype=BF16, seq_rows=seq)
        pa_s, pb_s = project(hs, tm=ts, tn=tn_s, dtype=F32, seq_rows=None)
        a_p = pool(pa_p, pb_p, hist16_p, hist_l=0, nb=1, rows=pool_tile, seq_tiles=seq // pool_tile,
                   pos0=0)
        a_s = pool(pa_s, pb_s, hist16_s, hist_l=l, nb=64, rows=SAMPLE_ROWS, seq_tiles=1, pos0=PAST_LEN)

        specs_p = _mlstm_specs(pb_p, gates_p, hist8_p, conv_w, conv_b3, mlstm_norm_w3, None, l=l,
                               hist_l=0, nseq=batch, rows=CHUNK, nc=nc, hcg_dtype=BF16,
                               where=lambda b, c: (b, c))
        specs_s = _mlstm_specs(pb_s, gates_s, hist8_s, conv_w, conv_b3, mlstm_norm_w3, state_s, l=l,
                               hist_l=l, nseq=dec_batch, rows=SAMPLE_ROWS, nc=1, hcg_dtype=F32,
                               where=lambda b, c: (b * nc + c, 0))
        kw_p = dict(rows=CHUNK, n_valid=CHUNK, nc=nc, zero_init=True)
        kw_s = dict(rows=SAMPLE_ROWS, n_valid=dec_seq, nc=1, zero_init=False)
        (hcg_p, *st_p), (hcg_s, *st_s) = _mlstm_pair((specs_p, kw_p), (specs_s, kw_s), st_p, st_s,
                                                     nseq_p=batch, nc_p=nc)

        if l < DEPTH - 1:
            xp, hp, gates_p = merge(xp, a_p, hcg_p, pb_p, pe_p)
            xs, hs, gates_s = merge(xs, a_s, hcg_s, pb_s, pe_s)
        else:
            xp = merge(xp, a_p, hcg_p, pb_p, pe_p)
            xs = merge(xs, a_s, hcg_s, pb_s, pe_s)

        pool_p.append(pa_p.reshape(batch, seq, POOL_WIDTH)[:, seq - POOL_HIST:])
        conv_p.append(tail_p[:, CONV_HALO - (CONV_W - 1):])
        sa = pa_s.reshape(dec_batch, SAMPLE_ROWS, POOL_WIDTH)
        sb = pb_s.reshape(dec_batch, SAMPLE_ROWS, PB_COLS)
        pool_s.append(jnp.concatenate([state_pool[l][:, dec_seq:], sa[:, :dec_seq]], axis=1))
        conv_s.append(sb[:, dec_seq - (CONV_W - 1):dec_seq, PB_Q:PB_Q + 2 * MLSTM_WIDTH])

    y_prompt = xp.reshape(batch, seq, D_MODEL)
    y_sample = xs.reshape(dec_batch, SAMPLE_ROWS, D_MODEL)[:, :dec_seq]
    c_p, n_p, m_p = st_p
    c_s, n_s, m_s = st_s
    return (y_prompt, y_sample, jnp.stack(pool_p), jnp.stack(pool_s), jnp.stack(conv_p),
            jnp.stack(conv_s), c_p, c_s, n_p[:, :, :, 0, :], n_s[:, :, :, 0, :],
            m_p[:, :, :, 0, 0], m_s[:, :, :, 0, 0])
```
